```python
import jax, jax.numpy as jnp
from jax import lax
import numpy as np

D_MODEL = 2048
BATCH = 4
SEQ = 2048
DEPTH = 4
DEC_BATCH = 128
DEC_SEQ = 4
PAST_LEN = 16384
PAGE_SIZE = 128

N_MIXERS = 2
N_GLA = (DEPTH + 1) // 2
N_SGU = DEPTH // 2

GLA_HEADS = 4
GLA_KEY_DIM = D_MODEL // 2
GLA_VAL_DIM = D_MODEL
GLA_DK = GLA_KEY_DIM // GLA_HEADS
GLA_DV = GLA_VAL_DIM // GLA_HEADS
GLA_GATE_RANK = 16
GLA_GATE_NORMALIZER = 16.0
GLA_CHUNK = 32
GLA_IN = 2 * GLA_KEY_DIM + 2 * GLA_VAL_DIM + GLA_GATE_RANK

SGU_DIM = D_MODEL
SGU_GROUPS = 8
SGU_GROUP_DIM = SGU_DIM // SGU_GROUPS
SGU_CHUNK = 128

N_EXPERTS = 32
TOP_K = 4
D_EXPERT = D_MODEL
SWIGLU_LIMIT = 7.0
SWIGLU_ALPHA = 1.702
MOE_BLOCK = 128

NORM_EPS = 1e-6

kernel_name = 'gla_sgu_moe_adaln_hybrid_step'


def rmsnorm(x, g):
    xf = x.astype(jnp.float32)
    y = xf * lax.rsqrt(jnp.mean(xf * xf, axis=-1, keepdims=True) + NORM_EPS)
    return (y * g.astype(jnp.float32)).astype(x.dtype)


def layernorm(x, g, b):
    xf = x.astype(jnp.float32)
    mu = jnp.mean(xf, axis=-1, keepdims=True)
    xc = xf - mu
    y = xc * lax.rsqrt(jnp.mean(xc * xc, axis=-1, keepdims=True) + NORM_EPS)
    return (y * g.astype(jnp.float32) + b.astype(jnp.float32)).astype(x.dtype)


def adaln(c, w, b):
    mod = (jax.nn.silu(c) @ w + b)[:, None, :]
    return jnp.split(mod, 6, axis=-1)


def modulate(h, shift, scale):
    return h * (1.0 + scale) + shift


def gla_chunk(S, q, k, v, lg):
    q = q.astype(jnp.float32)
    k = k.astype(jnp.float32)
    v = v.astype(jnp.float32)
    C = q.shape[1]
    b = jnp.cumsum(lg, axis=1)
    q_dec = q * jnp.exp(b)
    k_inv = k * jnp.exp(-b)
    causal = jnp.tril(jnp.ones((C, C), dtype=bool))
    att = jnp.where(causal, jnp.einsum('nthd,nshd->nhts', q_dec, k_inv), 0.0)
    o = jnp.einsum('nhts,nshv->nthv', att, v) + jnp.einsum('nthd,nhdv->nthv', q_dec, S)
    b_last = b[:, -1]
    k_to_end = k * jnp.exp(b_last[:, None] - b)
    S_new = S * jnp.exp(b_last)[..., None] + jnp.einsum('nshd,nshv->nhdv', k_to_end, v)
    return S_new, o


def gla_mixer(h, S0, w_in, w_gate2, b_gate, onorm_g, w_out, chunk):
    N, T, _ = h.shape
    z = h @ w_in
    q, k, v, g, a = jnp.split(z, [GLA_KEY_DIM, 2 * GLA_KEY_DIM, 2 * GLA_KEY_DIM + GLA_VAL_DIM,
                                  2 * GLA_KEY_DIM + 2 * GLA_VAL_DIM], axis=-1)
    q = q.reshape(N, T, GLA_HEADS, GLA_DK) * (GLA_DK ** -0.5)
    k = k.reshape(N, T, GLA_HEADS, GLA_DK)
    v = v.reshape(N, T, GLA_HEADS, GLA_DV)
    lg = (jax.nn.log_sigmoid((a @ w_gate2 + b_gate).astype(jnp.float32)) / GLA_GATE_NORMALIZER)
    lg = lg.reshape(N, T, GLA_HEADS, GLA_DK)
    nc = T // chunk
    to_chunks = lambda t: t.reshape(N, nc, chunk, *t.shape[2:]).swapaxes(0, 1)
    S_fin, o = lax.scan(lambda S, xs: gla_chunk(S, *xs), S0.astype(jnp.float32),
                        (to_chunks(q), to_chunks(k), to_chunks(v), to_chunks(lg)))
    o = o.swapaxes(0, 1).reshape(N, T, GLA_HEADS, GLA_DV)
    o = rmsnorm(o, onorm_g).reshape(N, T, GLA_VAL_DIM).astype(h.dtype) * jax.nn.silu(g)
    return o @ w_out, S_fin


def sgu_mixer(h, w_in, b_in, ln_g, ln_b, w_s, b_s, w_out, b_out):
    N, T, _ = h.shape
    L = min(T, SGU_CHUNK)
    nc = T // L
    z = jax.nn.gelu(h @ w_in + b_in, approximate=False)
    u, v = jnp.split(z, 2, axis=-1)
    v = layernorm(v, ln_g, ln_b)
    vg = v.reshape(N, nc, L, SGU_GROUPS, SGU_GROUP_DIM)
    ws = jnp.where(jnp.tril(jnp.ones((L, L), dtype=bool)), w_s[:, :L, :L], 0.0)
    s = jnp.einsum('gts,nlsgc->nltgc', ws, vg) + b_s[:, :L].T[:, :, None]
    s = s.reshape(N, T, SGU_DIM)
    return (u * s) @ w_out + b_out, v


def moe_ffn(h, w_router, b_router, w_gu, b_gu, w_down, b_down):
    T, D = h.shape
    TK = T * TOP_K
    logits = h.astype(jnp.float32) @ w_router.astype(jnp.float32) + b_router.astype(jnp.float32)
    top_val, top_idx = lax.top_k(logits, TOP_K)
    gate_w = jax.nn.softmax(top_val, axis=-1).reshape(TK)
    flat_e = top_idx.reshape(TK)
    flat_tok = jnp.repeat(jnp.arange(T, dtype=jnp.int32), TOP_K)
    order = jnp.argsort(flat_e, stable=True)
    sorted_e = flat_e[order]
    sorted_tok = flat_tok[order]
    sorted_w = gate_w[order]
    counts = jnp.bincount(flat_e, length=N_EXPERTS)
    padded = (counts + MOE_BLOCK - 1) // MOE_BLOCK * MOE_BLOCK
    start = jnp.cumsum(counts) - counts
    pad_end = jnp.cumsum(padded)
    pad_start = pad_end - padded
    dest = pad_start[sorted_e] + jnp.arange(TK, dtype=jnp.int32) - start[sorted_e]
    n_blocks = -(-(TK + N_EXPERTS * (MOE_BLOCK - 1)) // MOE_BLOCK)
    n_rows = n_blocks * MOE_BLOCK
    row_tok = jnp.full((n_rows,), T, jnp.int32).at[dest].set(sorted_tok)
    block_e = jnp.minimum(jnp.searchsorted(pad_end, jnp.arange(n_blocks, dtype=jnp.int32) * MOE_BLOCK,
                                           side='right'), N_EXPERTS - 1)
    h_pad = jnp.concatenate([h, jnp.zeros((1, D), h.dtype)], axis=0)
    xb = h_pad[row_tok].reshape(n_blocks, MOE_BLOCK, D)

    def expert_block(args):
        xblk, e = args
        gu = xblk @ w_gu[e] + b_gu[e]
        gate, up = jnp.split(gu, 2, axis=-1)
        gate = jnp.minimum(gate, SWIGLU_LIMIT)
        up = jnp.clip(up, -SWIGLU_LIMIT, SWIGLU_LIMIT)
        glu = gate * jax.nn.sigmoid(gate * SWIGLU_ALPHA)
        return ((up + 1.0) * glu) @ w_down[e] + b_down[e]

    yb = lax.map(expert_block, (xb, block_e)).reshape(n_rows, D)
    return jnp.zeros((T, D), h.dtype).at[sorted_tok].add(yb[dest] * sorted_w[:, None].astype(h.dtype))


def setup_inputs(seed: int = 0) -> dict:
    key = jax.random.key(seed)
    ks = iter(jax.random.split(key, 40))

    def nrm(shape, scale):
        return jax.random.normal(next(ks), shape, jnp.float32) * scale

    def gain(shape):
        return 1.0 + nrm(shape, 0.02)

    D = D_MODEL
    return {
        'x_prompt': nrm((BATCH, SEQ, D), 1.0),
        'x_sample': nrm((DEC_BATCH, DEC_SEQ, D), 1.0),
        'state_gla': nrm((N_GLA, DEC_BATCH, GLA_HEADS, GLA_DK, GLA_DV), 1.0),
        'c_prompt': nrm((BATCH, D), 1.0),
        'c_sample': nrm((DEC_BATCH, D), 1.0),
        'ada_w': nrm((DEPTH, D, 6 * D), 0.5 * D ** -0.5),
        'ada_b': nrm((DEPTH, 6 * D), 0.02),
        'norm1_g': gain((DEPTH, D)),
        'norm2_g': gain((DEPTH, D)),
        'final_g': gain((D,)),
        'gla_w_in': nrm((N_GLA, D, GLA_IN), D ** -0.5),
        'gla_w_gate2': nrm((N_GLA, GLA_GATE_RANK, GLA_KEY_DIM), GLA_GATE_RANK ** -0.5),
        'gla_b_gate': nrm((N_GLA, GLA_KEY_DIM), 0.1),
        'gla_onorm_g': gain((N_GLA, GLA_DV)),
        'gla_w_out': nrm((N_GLA, GLA_VAL_DIM, D), GLA_VAL_DIM ** -0.5),
        'sgu_w_in': nrm((N_SGU, D, 2 * SGU_DIM), D ** -0.5),
        'sgu_b_in': nrm((N_SGU, 2 * SGU_DIM), 0.02),
        'sgu_ln_g': gain((N_SGU, SGU_DIM)),
        'sgu_ln_b': nrm((N_SGU, SGU_DIM), 0.02),
        'sgu_w_s': nrm((N_SGU, SGU_GROUPS, SGU_CHUNK, SGU_CHUNK), SGU_CHUNK ** -0.5),
        'sgu_b_s': 1.0 + nrm((N_SGU, SGU_GROUPS, SGU_CHUNK), 0.02),
        'sgu_w_out': nrm((N_SGU, SGU_DIM, D), SGU_DIM ** -0.5),
        'sgu_b_out': nrm((N_SGU, D), 0.02),
        'router_w': nrm((DEPTH, D, N_EXPERTS), D ** -0.5),
        'router_b': nrm((DEPTH, N_EXPERTS), 0.01),
        'exp_w_gu': nrm((DEPTH, N_EXPERTS, D, 2 * D_EXPERT), D ** -0.5),
        'exp_b_gu': nrm((DEPTH, N_EXPERTS, 2 * D_EXPERT), 0.01),
        'exp_w_down': nrm((DEPTH, N_EXPERTS, D_EXPERT, D), D_EXPERT ** -0.5),
        'exp_b_down': nrm((DEPTH, N_EXPERTS, D), 0.01),
    }


def reference(x_prompt, x_sample, state_gla, c_prompt, c_sample, ada_w, ada_b, norm1_g, norm2_g, final_g,
              gla_w_in, gla_w_gate2, gla_b_gate, gla_onorm_g, gla_w_out,
              sgu_w_in, sgu_b_in, sgu_ln_g, sgu_ln_b, sgu_w_s, sgu_b_s, sgu_w_out, sgu_b_out,
              router_w, router_b, exp_w_gu, exp_b_gu, exp_w_down, exp_b_down):
    xp, xs = x_prompt, x_sample
    n_p = xp.shape[0] * xp.shape[1]
    new_gla_p, new_gla_s, new_sgu_s = [], [], []
    for i in range(DEPTH):
        mp = adaln(c_prompt, ada_w[i], ada_b[i])
        ms = adaln(c_sample, ada_w[i], ada_b[i])
        hp = modulate(rmsnorm(xp, norm1_g[i]), mp[0], mp[1])
        hs = modulate(rmsnorm(xs, norm1_g[i]), ms[0], ms[1])
        j = i // N_MIXERS
        if i % N_MIXERS == 0:
            S0 = jnp.zeros((xp.shape[0], GLA_HEADS, GLA_DK, GLA_DV), jnp.float32)
            op, sp = gla_mixer(hp, S0, gla_w_in[j], gla_w_gate2[j], gla_b_gate[j], gla_onorm_g[j],
                               gla_w_out[j], GLA_CHUNK)
            os_, ss = gla_mixer(hs, state_gla[j], gla_w_in[j], gla_w_gate2[j], gla_b_gate[j],
                                gla_onorm_g[j], gla_w_out[j], hs.shape[1])
            new_gla_p.append(sp)
            new_gla_s.append(ss)
        else:
            op, _ = sgu_mixer(hp, sgu_w_in[j], sgu_b_in[j], sgu_ln_g[j], sgu_ln_b[j], sgu_w_s[j],
                              sgu_b_s[j], sgu_w_out[j], sgu_b_out[j])
            os_, vs = sgu_mixer(hs, sgu_w_in[j], sgu_b_in[j], sgu_ln_g[j], sgu_ln_b[j], sgu_w_s[j],
                                sgu_b_s[j], sgu_w_out[j], sgu_b_out[j])
            new_sgu_s.append(vs)
        xp = xp + mp[2] * op
        xs = xs + ms[2] * os_
        hp = modulate(rmsnorm(xp, norm2_g[i]), mp[3], mp[4])
        hs = modulate(rmsnorm(xs, norm2_g[i]), ms[3], ms[4])
        h_all = jnp.concatenate([hp.reshape(-1, D_MODEL), hs.reshape(-1, D_MODEL)], axis=0)
        f = moe_ffn(h_all, router_w[i], router_b[i], exp_w_gu[i], exp_b_gu[i], exp_w_down[i], exp_b_down[i])
        xp = xp + mp[5] * f[:n_p].reshape(xp.shape)
        xs = xs + ms[5] * f[n_p:].reshape(xs.shape)
    y_prompt = rmsnorm(xp, final_g)
    y_sample = rmsnorm(xs, final_g)
    return (y_prompt, y_sample, jnp.stack(new_gla_p), jnp.stack(new_gla_s), jnp.stack(new_sgu_s))
```

```python
import functools

import jax
import jax.numpy as jnp
from jax import lax
from jax.experimental import pallas as pl
from jax.experimental.pallas import tpu as pltpu

F32 = jnp.float32
BF16 = jnp.bfloat16

GLA_HEADS = 4
GLA_GATE_RANK = 16
GLA_GATE_NORMALIZER = 16.0
GLA_CHUNK = 32
SGU_GROUPS = 8
SGU_CHUNK = 128
N_EXPERTS = 32
TOP_K = 4
SWIGLU_LIMIT = 7.0
SWIGLU_ALPHA = 1.702
NORM_EPS = 1e-6

LANES = 128
SUBLANES = 8
VMEM_LIMIT = 56 * 1024 * 1024
ROW_TILE = 512
MOE_ROWS = 256
HI = lax.Precision.HIGHEST


def _cparams(*sem):
    return pltpu.CompilerParams(dimension_semantics=sem, vmem_limit_bytes=VMEM_LIMIT)


def _tile(n, target):
    t = min(n, target) // LANES * LANES
    while n % t:
        t -= LANES
    return t


def _bdot(a, b):
    return jnp.dot(a.astype(BF16), b.astype(BF16), preferred_element_type=F32)


def _silu(x):
    return x * (1.0 / (1.0 + jnp.exp(-x)))


def _adaln_kernel(c_ref, w_ref, b_ref, o_ref):
    o_ref[...] = _bdot(_silu(c_ref[...]), w_ref[...]) + b_ref[...]


def _adaln_all(c_all, ada_w, ada_b, bn):
    depth, d, n6 = ada_w.shape
    rows = c_all.shape[0]
    return pl.pallas_call(
        _adaln_kernel,
        out_shape=jax.ShapeDtypeStruct((depth, rows, n6), F32),
        grid=(depth, n6 // bn),
        in_specs=[
            pl.BlockSpec((rows, d), lambda l, n: (0, 0)),
            pl.BlockSpec((None, d, bn), lambda l, n: (l, 0, n)),
            pl.BlockSpec((None, 1, bn), lambda l, n: (l, 0, n)),
        ],
        out_specs=pl.BlockSpec((None, rows, bn), lambda l, n: (l, 0, n)),
        compiler_params=_cparams("arbitrary", "arbitrary"),
        name="adaln",
    )(c_all, ada_w, ada_b.reshape(depth, 1, n6))


class _Rows:
    def __init__(self, batch, seq, m_sample, rt):
        self.rt = rt
        self.halved = lambda: _Rows(batch, seq, m_sample, rt // 2)
        self.tiles_per_batch = seq // rt
        self.n_prompt_tiles = batch * seq // rt
        self.n_tiles = self.n_prompt_tiles + m_sample // rt
        self.batch = batch

    def mp_spec(self, which, width, col=None):
        tpb, nb = self.tiles_per_batch, self.batch
        if col is None:
            return pl.BlockSpec((None, None, 1, width), lambda i: (jnp.minimum(i // tpb, nb - 1), which, 0, 0))
        return pl.BlockSpec((None, None, 1, width),
                            lambda n, i: (jnp.minimum(i // tpb, nb - 1), which, 0, n))

    def ms_spec(self, which, width, col=None):
        npt = self.n_prompt_tiles
        if col is None:
            return pl.BlockSpec((None, self.rt, width), lambda i: (which, jnp.maximum(i - npt, 0), 0))
        return pl.BlockSpec((None, self.rt, width), lambda n, i: (which, jnp.maximum(i - npt, 0), n))


def _pick_mod(i, rows, mp_ref, ms_ref):
    return jnp.where(i >= rows.n_prompt_tiles, ms_ref[...], mp_ref[...])


def _rms(x, g):
    return x * lax.rsqrt(jnp.mean(x * x, axis=-1, keepdims=True) + NORM_EPS) * g


def _norm_mod_kernel(rows, x_ref, g_ref, mp_sh, mp_sc, ms_sh, ms_sc, h_ref):
    i = pl.program_id(0)
    y = _rms(x_ref[...], g_ref[...])
    h = y * (1.0 + _pick_mod(i, rows, mp_sc, ms_sc)) + _pick_mod(i, rows, mp_sh, ms_sh)
    h_ref[...] = h.astype(h_ref.dtype)


def _norm_mod(rows, x, g, mod_p, mod_s, which_shift, which_scale):
    m, d = x.shape
    rt = rows.rt
    return pl.pallas_call(
        functools.partial(_norm_mod_kernel, rows),
        out_shape=jax.ShapeDtypeStruct((m, d), BF16),
        grid=(rows.n_tiles,),
        in_specs=[
            pl.BlockSpec((rt, d), lambda i: (i, 0)),
            pl.BlockSpec((1, d), lambda i: (0, 0)),
            rows.mp_spec(which_shift, d), rows.mp_spec(which_scale, d),
            rows.ms_spec(which_shift, d), rows.ms_spec(which_scale, d),
        ],
        out_specs=pl.BlockSpec((rt, d), lambda i: (i, 0)),
        compiler_params=_cparams("arbitrary"),
        name="norm_mod",
    )(x, g.reshape(1, d), mod_p, mod_p, mod_s, mod_s)


def _lane_pick(lane, k, col, acc):
    return jnp.where(lane == k, col, acc)


def _norm_route_kernel(rows, x_ref, g_ref, mp_sh, mp_sc, ms_sh, ms_sc, rw_ref, rb_ref,
                       h_ref, idx_ref, gw_ref, rank_ref, cnt_ref, carry_ref):
    i = pl.program_id(0)
    rt = rows.rt

    @pl.when(i == 0)
    def _():
        carry_ref[...] = jnp.zeros_like(carry_ref)

    y = _rms(x_ref[...], g_ref[...])
    h = y * (1.0 + _pick_mod(i, rows, mp_sc, ms_sc)) + _pick_mod(i, rows, mp_sh, ms_sh)
    h_ref[...] = h.astype(h_ref.dtype)

    logits = jnp.dot(h, rw_ref[...], preferred_element_type=F32, precision=HI) + rb_ref[...]
    lane = lax.broadcasted_iota(jnp.int32, (rt, LANES), 1)
    neg = jnp.float32(-jnp.inf)
    work = jnp.where(lane < N_EXPERTS, logits, neg)
    member = jnp.zeros((rt, LANES), F32)
    vals, idxs = [], []
    for _ in range(TOP_K):
        mx = jnp.max(work, axis=-1, keepdims=True)
        idx = jnp.min(jnp.where(work == mx, lane, LANES), axis=-1, keepdims=True)
        sel = lane == idx
        member = jnp.where(sel, 1.0, member)
        work = jnp.where(sel, neg, work)
        vals.append(mx)
        idxs.append(idx)
    exps = [jnp.exp(v - vals[0]) for v in vals]
    denom = exps[0]
    for e in exps[1:]:
        denom = denom + e

    r_i = lax.broadcasted_iota(jnp.int32, (rt, rt), 0)
    c_i = lax.broadcasted_iota(jnp.int32, (rt, rt), 1)
    below = jnp.where(r_i > c_i, 1.0, 0.0).astype(BF16)
    before = jnp.dot(below, member.astype(BF16), preferred_element_type=F32) + carry_ref[0:1, :]

    idx_out = jnp.zeros((rt, LANES), jnp.int32)
    gw_out = jnp.zeros((rt, LANES), F32)
    rank_out = jnp.zeros((rt, LANES), jnp.int32)
    for k in range(TOP_K):
        rank_k = jnp.sum(jnp.where(lane == idxs[k], before, 0.0), axis=-1, keepdims=True)
        idx_out = _lane_pick(lane, k, idxs[k], idx_out)
        gw_out = _lane_pick(lane, k, exps[k] / denom, gw_out)
        rank_out = _lane_pick(lane, k, rank_k.astype(jnp.int32), rank_out)
    idx_ref[...] = idx_out
    gw_ref[...] = gw_out
    rank_ref[...] = rank_out

    carry_ref[...] = carry_ref[...] + jnp.sum(member, axis=0, keepdims=True)
    cnt_ref[...] = carry_ref[...]


def _norm_route(rows, x, g, mod_p, mod_s, router_w, router_b):
    m, d = x.shape
    rt = rows.rt
    rw = jnp.pad(router_w, ((0, 0), (0, LANES - N_EXPERTS)))
    rb = jnp.pad(router_b, (0, LANES - N_EXPERTS)).reshape(1, LANES)
    row_spec = pl.BlockSpec((rt, LANES), lambda i: (i, 0))
    return pl.pallas_call(
        functools.partial(_norm_route_kernel, rows),
        out_shape=(
            jax.ShapeDtypeStruct((m, d), BF16),
            jax.ShapeDtypeStruct((m, LANES), jnp.int32),
            jax.ShapeDtypeStruct((m, LANES), F32),
            jax.ShapeDtypeStruct((m, LANES), jnp.int32),
            jax.ShapeDtypeStruct((SUBLANES, LANES), F32),
        ),
        grid=(rows.n_tiles,),
        in_specs=[
            pl.BlockSpec((rt, d), lambda i: (i, 0)),
            pl.BlockSpec((1, d), lambda i: (0, 0)),
            rows.mp_spec(3, d), rows.mp_spec(4, d),
            rows.ms_spec(3, d), rows.ms_spec(4, d),
            pl.BlockSpec((d, LANES), lambda i: (0, 0)),
            pl.BlockSpec((1, LANES), lambda i: (0, 0)),
        ],
        out_specs=(
            pl.BlockSpec((rt, d), lambda i: (i, 0)),
            row_spec, row_spec, row_spec,
            pl.BlockSpec((SUBLANES, LANES), lambda i: (0, 0)),
        ),
        scratch_shapes=[pltpu.VMEM((SUBLANES, LANES), F32)],
        compiler_params=_cparams("arbitrary"),
        name="norm_route",
    )(x, g.reshape(1, d), mod_p, mod_p, mod_s, mod_s, rw, rb)


def _gelu(x):
    return 0.5 * x * (1.0 + lax.erf(x * (2.0 ** -0.5)))


def _mm_kernel(rows, epilogue, has_bias, *refs):
    refs = list(refs)
    x_ref, w_ref = refs[0], refs[1]
    pos = 2
    b_ref = None
    if has_bias:
        b_ref = refs[pos]
        pos += 1
    if epilogue == "resid":
        res_ref, mp_g, ms_g = refs[pos:pos + 3]
        pos += 3
    o_ref, wbf_ref = refs[pos], refs[pos + 1]
    i = pl.program_id(1)

    @pl.when(i == 0)
    def _():
        wbf_ref[...] = w_ref[...].astype(BF16)

    acc = jnp.dot(x_ref[...].astype(BF16), wbf_ref[...], preferred_element_type=F32)
    if has_bias:
        acc = acc + b_ref[...]
    if epilogue == "gelu":
        acc = _gelu(acc)
    elif epilogue == "resid":
        acc = res_ref[...] + _pick_mod(i, rows, mp_g, ms_g) * acc
    o_ref[...] = acc.astype(o_ref.dtype)


def _matmul(rows, x, w, bias=None, epilogue="plain", resid=None, mod_p=None, mod_s=None, which_gate=None,
            bn=1024, out_dtype=F32, name="matmul"):
    m, k = x.shape
    n = w.shape[1]
    bn = _tile(n, bn)
    rt = rows.rt
    in_specs = [pl.BlockSpec((rt, k), lambda j, i: (i, 0)), pl.BlockSpec((k, bn), lambda j, i: (0, j))]
    args = [x, w]
    if bias is not None:
        in_specs.append(pl.BlockSpec((1, bn), lambda j, i: (0, j)))
        args.append(bias.reshape(1, n))
    if epilogue == "resid":
        in_specs += [pl.BlockSpec((rt, bn), lambda j, i: (i, j)),
                     rows.mp_spec(which_gate, bn, col=True), rows.ms_spec(which_gate, bn, col=True)]
        args += [resid, mod_p, mod_s]
    return pl.pallas_call(
        functools.partial(_mm_kernel, rows, epilogue, bias is not None),
        out_shape=jax.ShapeDtypeStruct((m, n), out_dtype),
        grid=(n // bn, m // rt),
        in_specs=in_specs,
        out_specs=pl.BlockSpec((rt, bn), lambda j, i: (i, j)),
        scratch_shapes=[pltpu.VMEM((k, bn), BF16)],
        compiler_params=_cparams("arbitrary", "arbitrary"),
        name=name,
    )(*args)


def _log_sigmoid(x):
    return jnp.minimum(x, 0.0) - jnp.log1p(jnp.exp(-jnp.abs(x)))


def _gla_gate_logs(a, w2, b):
    return _log_sigmoid(_bdot(a, w2) + b) * (1.0 / GLA_GATE_NORMALIZER)


def _gla_chunk(q, k, v, lg, s, scale):
    c, dk = q.shape
    dv = v.shape[1]
    r_i = lax.broadcasted_iota(jnp.int32, (c, c), 0)
    c_i = lax.broadcasted_iota(jnp.int32, (c, c), 1)
    causal = r_i >= c_i
    b = jnp.dot(jnp.where(causal, 1.0, 0.0), lg, preferred_element_type=F32, precision=HI)
    b_end_col = lax.dot_general(lg, jnp.ones((c, LANES), F32), (((0,), (0,)), ((), ())),
                                preferred_element_type=F32, precision=HI)
    b_end_row = b[c - 1:c, :]
    q_dec = (q * scale) * jnp.exp(b)
    k_inv = k * jnp.exp(-b)
    k_end = k * jnp.exp(b_end_row - b)
    att = lax.dot_general(q_dec.astype(BF16), k_inv.astype(BF16), (((1,), (1,)), ((), ())),
                          preferred_element_type=F32)
    att = jnp.where(causal, att, 0.0)
    o = _bdot(att, v) + _bdot(q_dec, s)
    decay = jnp.exp(b_end_col)
    decay = jnp.concatenate([decay] * (dv // LANES), axis=1)
    kv = lax.dot_general(k_end.astype(BF16), v.astype(BF16), (((0,), (0,)), ((), ())),
                         preferred_element_type=F32)
    return o, s * decay + kv


def _gla_out(o, g, onorm_g):
    return _rms(o, onorm_g) * _silu(g)


def _gla_prompt_kernel(dims, q_ref, k_ref, v_ref, g_ref, a_ref, w2_ref, bg_ref, on_ref,
                       og_ref, sout_ref, s_ref, lg_ref):
    heads, dk, dv, chunk, ts = dims
    step = pl.program_id(1)

    @pl.when(step == 0)
    def _():
        s_ref[...] = jnp.zeros_like(s_ref)

    lg_ref[...] = _gla_gate_logs(a_ref[...], w2_ref[...], bg_ref[...])
    scale = dk ** -0.5

    def body(ci, carry):
        r0 = pl.multiple_of(ci * chunk, chunk)
        rsl = pl.ds(r0, chunk)
        for h in range(heads):
            ksl = slice(h * dk, (h + 1) * dk)
            vsl = slice(h * dv, (h + 1) * dv)
            o, s_new = _gla_chunk(q_ref[rsl, ksl], k_ref[rsl, ksl], v_ref[rsl, vsl], lg_ref[rsl, ksl],
                                  s_ref[h], scale)
            s_ref[h] = s_new
            og_ref[rsl, vsl] = _gla_out(o, g_ref[rsl, vsl], on_ref[...]).astype(og_ref.dtype)
        return carry

    lax.fori_loop(0, ts // chunk, body, 0)

    @pl.when(step == pl.num_programs(1) - 1)
    def _():
        sout_ref[...] = s_ref[...]


def _gla_prompt(z, a, w2p, b_gate, onorm_g, batch, seq, m_total, ts):
    d2 = z.shape[1]
    d = d2 // 3
    kd = d // 2
    heads = GLA_HEADS
    dk, dv = kd // heads, d // heads
    spb = seq // ts
    dims = (heads, dk, dv, GLA_CHUNK, ts)
    row = lambda n, s: n * spb + s
    return pl.pallas_call(
        functools.partial(_gla_prompt_kernel, dims),
        out_shape=(jax.ShapeDtypeStruct((m_total, d), F32),
                   jax.ShapeDtypeStruct((batch, heads, dk, dv), F32)),
        grid=(batch, spb),
        in_specs=[
            pl.BlockSpec((ts, kd), lambda n, s: (row(n, s), 0)),
            pl.BlockSpec((ts, kd), lambda n, s: (row(n, s), 1)),
            pl.BlockSpec((ts, d), lambda n, s: (row(n, s), 1)),
            pl.BlockSpec((ts, d), lambda n, s: (row(n, s), 2)),
            pl.BlockSpec((ts, LANES), lambda n, s: (row(n, s), 0)),
            pl.BlockSpec((LANES, kd), lambda n, s: (0, 0)),
            pl.BlockSpec((1, kd), lambda n, s: (0, 0)),
            pl.BlockSpec((1, dv), lambda n, s: (0, 0)),
        ],
        out_specs=(pl.BlockSpec((ts, d), lambda n, s: (row(n, s), 0)),
                   pl.BlockSpec((None, heads, dk, dv), lambda n, s: (n, 0, 0, 0))),
        scratch_shapes=[pltpu.VMEM((heads, dk, dv), F32), pltpu.VMEM((ts, kd), F32)],
        compiler_params=_cparams("arbitrary", "arbitrary"),
        name="gla_prompt",
    )(z, z, z, z, a, w2p, b_gate.reshape(1, kd), onorm_g.reshape(1, dv))


def _gla_sample_kernel(dims, n_alias, q_ref, k_ref, v_ref, g_ref, a_ref, w2_ref, bg_ref, on_ref, s0_ref, *refs):
    og_ref, sout_ref = refs[n_alias:]
    heads, dk, dv, nb, dec = dims
    c = nb * dec
    scale = dk ** -0.5
    lg_all = _gla_gate_logs(a_ref[...], w2_ref[...], bg_ref[...])
    row = lax.broadcasted_iota(jnp.int32, (c, 1), 0)
    for h in range(heads):
        ksl = slice(h * dk, (h + 1) * dk)
        vsl = slice(h * dv, (h + 1) * dv)
        q, k, v, lg = q_ref[:, ksl], k_ref[:, ksl], v_ref[:, vsl], lg_all[:, ksl]
        o_all = jnp.zeros((c, dv), F32)
        for b in range(nb):
            mine = (row >= b * dec) & (row < (b + 1) * dec)
            z = lambda t: jnp.where(mine, t, 0.0)
            o, s_new = _gla_chunk(z(q), z(k), z(v), z(lg), s0_ref[b, h], scale)
            sout_ref[b, h] = s_new
            o_all = jnp.where(mine, o, o_all)
        og_ref[:, vsl] = _gla_out(o_all, g_ref[:, vsl], on_ref[...]).astype(og_ref.dtype)


def _gla_sample(z, a, w2p, b_gate, onorm_g, state_all, j, og, s_prev, m_prompt, dec_batch, dec_seq, nb):
    d2 = z.shape[1]
    d = d2 // 3
    kd = d // 2
    heads = GLA_HEADS
    dk, dv = kd // heads, d // heads
    c = nb * dec_seq
    r0 = m_prompt // c
    dims = (heads, dk, dv, nb, dec_seq)
    st_spec = pl.BlockSpec((None, nb, heads, dk, dv), lambda i: (j, i, 0, 0, 0))
    in_specs = [
        pl.BlockSpec((c, kd), lambda i: (r0 + i, 0)),
        pl.BlockSpec((c, kd), lambda i: (r0 + i, 1)),
        pl.BlockSpec((c, d), lambda i: (r0 + i, 1)),
        pl.BlockSpec((c, d), lambda i: (r0 + i, 2)),
        pl.BlockSpec((c, LANES), lambda i: (r0 + i, 0)),
        pl.BlockSpec((LANES, kd), lambda i: (0, 0)),
        pl.BlockSpec((1, kd), lambda i: (0, 0)),
        pl.BlockSpec((1, dv), lambda i: (0, 0)),
        st_spec,
        pl.BlockSpec(memory_space=pl.ANY),
    ]
    args = [z, z, z, z, a, w2p, b_gate.reshape(1, kd), onorm_g.reshape(1, dv), state_all, og]
    aliases = {len(args) - 1: 0}
    if s_prev is not None:
        in_specs.append(pl.BlockSpec(memory_space=pl.ANY))
        args.append(s_prev)
        aliases[len(args) - 1] = 1
    return pl.pallas_call(
        functools.partial(_gla_sample_kernel, dims, len(aliases)),
        out_shape=(jax.ShapeDtypeStruct(og.shape, og.dtype),
                   jax.ShapeDtypeStruct(state_all.shape, state_all.dtype)),
        grid=(dec_batch // nb,),
        in_specs=in_specs,
        out_specs=(pl.BlockSpec((c, d), lambda i: (r0 + i, 0)), st_spec),
        input_output_aliases=aliases,
        compiler_params=_cparams("arbitrary"),
        name="gla_sample",
    )(*args)


def _sgu_kernel(groups, u_ref, v_ref, lg_ref, lb_ref, w_ref, bias_ref, us_ref, vn_ref):
    v = v_ref[...]
    mu = jnp.mean(v, axis=-1, keepdims=True)
    vc = v - mu
    vn = vc * lax.rsqrt(jnp.mean(vc * vc, axis=-1, keepdims=True) + NORM_EPS) * lg_ref[...] + lb_ref[...]
    vn_ref[...] = vn
    gd = v.shape[1] // groups
    for g in range(groups):
        sl = slice(g * gd, (g + 1) * gd)
        s = _bdot(w_ref[g], vn[:, sl]) + bias_ref[:, sl]
        us_ref[:, sl] = (u_ref[:, sl] * s).astype(us_ref.dtype)


def _sgu_spatial(zs, ln_g, ln_b, wmix, biasfull, m_prompt, m_sample):
    m, d2 = zs.shape
    sd = d2 // 2
    ch = SGU_CHUNK
    npc = m_prompt // ch
    which = lambda c: jnp.where(c >= npc, 1, 0)
    return pl.pallas_call(
        functools.partial(_sgu_kernel, SGU_GROUPS),
        out_shape=(jax.ShapeDtypeStruct((m, sd), BF16), jax.ShapeDtypeStruct((m_sample, sd), F32)),
        grid=(m // ch,),
        in_specs=[
            pl.BlockSpec((ch, sd), lambda c: (c, 0)),
            pl.BlockSpec((ch, sd), lambda c: (c, 1)),
            pl.BlockSpec((1, sd), lambda c: (0, 0)),
            pl.BlockSpec((1, sd), lambda c: (0, 0)),
            pl.BlockSpec((None, SGU_GROUPS, ch, ch), lambda c: (which(c), 0, 0, 0)),
            pl.BlockSpec((None, ch, sd), lambda c: (which(c), 0, 0)),
        ],
        out_specs=(pl.BlockSpec((ch, sd), lambda c: (c, 0)),
                   pl.BlockSpec((ch, sd), lambda c: (jnp.maximum(c - npc, 0), 0))),
        compiler_params=_cparams("arbitrary"),
        name="sgu_spatial",
    )(zs, zs, ln_g.reshape(1, sd), ln_b.reshape(1, sd), wmix, biasfull)


def _sgu_mix_tables(w_s, b_s, dec_seq):
    groups, ch, _ = w_s.shape
    tril = jnp.tril(jnp.ones((ch, ch), bool))
    wp = jnp.where(tril, w_s, 0.0)
    small = jnp.where(jnp.tril(jnp.ones((dec_seq, dec_seq), bool)), w_s[:, :dec_seq, :dec_seq], 0.0)
    reps = ch // dec_seq
    eye = jnp.eye(reps, dtype=w_s.dtype)
    ws = jnp.einsum("ab,gts->gatbs", eye, small).reshape(groups, ch, ch)
    bp = b_s.T
    bs = jnp.tile(b_s[:, :dec_seq].T, (reps, 1))
    return jnp.stack([wp, ws]), jnp.stack([bp, bs])


def _moe_gu_kernel(be_ref, nu_ref, x_ref, wg_ref, wu_ref, bg_ref, bu_ref, hid_ref, wg_bf, wu_bf):
    b = pl.program_id(1)
    prev = be_ref[jnp.maximum(b - 1, 0)]
    changed = jnp.logical_or(b == 0, be_ref[b] != prev)

    @pl.when(jnp.logical_and(changed, b < nu_ref[0]))
    def _():
        wg_bf[...] = wg_ref[...].astype(BF16)
        wu_bf[...] = wu_ref[...].astype(BF16)

    @pl.when(b < nu_ref[0])
    def _():
        x = x_ref[...]
        gate = jnp.dot(x, wg_bf[...], preferred_element_type=F32) + bg_ref[...]
        up = jnp.dot(x, wu_bf[...], preferred_element_type=F32) + bu_ref[...]
        gate = jnp.minimum(gate, SWIGLU_LIMIT)
        up = jnp.clip(up, -SWIGLU_LIMIT, SWIGLU_LIMIT)
        glu = gate * (1.0 / (1.0 + jnp.exp(-gate * SWIGLU_ALPHA)))
        hid_ref[...] = ((up + 1.0) * glu).astype(hid_ref.dtype)

    @pl.when(b >= nu_ref[0])
    def _():
        hid_ref[...] = jnp.zeros_like(hid_ref)


def _moe_down_kernel(be_ref, nu_ref, h_ref, w_ref, b_ref, y_ref, w_bf):
    b = pl.program_id(1)
    prev = be_ref[jnp.maximum(b - 1, 0)]
    changed = jnp.logical_or(b == 0, be_ref[b] != prev)

    @pl.when(jnp.logical_and(changed, b < nu_ref[0]))
    def _():
        w_bf[...] = w_ref[...].astype(BF16)

    @pl.when(b < nu_ref[0])
    def _():
        y_ref[...] = jnp.dot(h_ref[...], w_bf[...], preferred_element_type=F32) + b_ref[...]

    @pl.when(b >= nu_ref[0])
    def _():
        y_ref[...] = jnp.zeros_like(y_ref)


def _moe_experts(x_sorted, block_e, n_used, w_gu, b_gu, w_down, b_down, bnh, bnd):
    n_rows, d = x_sorted.shape
    n_exp, _, de2 = w_gu.shape
    de = de2 // 2
    bm = MOE_ROWS
    n_blocks = n_rows // bm
    nj = de // bnh
    b_gu3 = b_gu.reshape(n_exp, 1, de2)
    hid = pl.pallas_call(
        _moe_gu_kernel,
        out_shape=jax.ShapeDtypeStruct((n_rows, de), BF16),
        grid_spec=pltpu.PrefetchScalarGridSpec(
            num_scalar_prefetch=2,
            grid=(nj, n_blocks),
            in_specs=[
                pl.BlockSpec((bm, d), lambda j, b, be, nu: (b, 0)),
                pl.BlockSpec((None, d, bnh), lambda j, b, be, nu: (be[b], 0, j)),
                pl.BlockSpec((None, d, bnh), lambda j, b, be, nu: (be[b], 0, nj + j)),
                pl.BlockSpec((None, 1, bnh), lambda j, b, be, nu: (be[b], 0, j)),
                pl.BlockSpec((None, 1, bnh), lambda j, b, be, nu: (be[b], 0, nj + j)),
            ],
            out_specs=pl.BlockSpec((bm, bnh), lambda j, b, be, nu: (b, j)),
            scratch_shapes=[pltpu.VMEM((d, bnh), BF16), pltpu.VMEM((d, bnh), BF16)],
        ),
        compiler_params=_cparams("arbitrary", "arbitrary"),
        name="moe_gate_up",
    )(block_e, n_used, x_sorted, w_gu, w_gu, b_gu3, b_gu3)

    njd = d // bnd
    return pl.pallas_call(
        _moe_down_kernel,
        out_shape=jax.ShapeDtypeStruct((n_rows, d), F32),
        grid_spec=pltpu.PrefetchScalarGridSpec(
            num_scalar_prefetch=2,
            grid=(njd, n_blocks),
            in_specs=[
                pl.BlockSpec((bm, de), lambda j, b, be, nu: (b, 0)),
                pl.BlockSpec((None, de, bnd), lambda j, b, be, nu: (be[b], 0, j)),
                pl.BlockSpec((None, 1, bnd), lambda j, b, be, nu: (be[b], 0, j)),
            ],
            out_specs=pl.BlockSpec((bm, bnd), lambda j, b, be, nu: (b, j)),
            scratch_shapes=[pltpu.VMEM((de, bnd), BF16)],
        ),
        compiler_params=_cparams("arbitrary", "arbitrary"),
        name="moe_down",
    )(block_e, n_used, hid, w_down, b_down.reshape(n_exp, 1, d))


def _combine_kernel(rows, final, *refs):
    x_ref, gw_ref, mp_g, ms_g = refs[0], refs[1], refs[2], refs[3]
    y_refs = refs[4:4 + TOP_K]
    pos = 4 + TOP_K
    if final:
        fg_ref, o_ref = refs[pos], refs[pos + 1]
    else:
        o_ref = refs[pos]
    i = pl.program_id(0)
    gw = gw_ref[...]
    f = y_refs[0][...] * gw[:, 0:1]
    for k in range(1, TOP_K):
        f = f + y_refs[k][...] * gw[:, k:k + 1]
    x_new = x_ref[...] + _pick_mod(i, rows, mp_g, ms_g) * f
    if final:
        x_new = _rms(x_new, fg_ref[...])
    o_ref[...] = x_new


def _combine(rows, x, gw, mod_p, mod_s, yk, final_g=None):
    m, d = x.shape
    rt = rows.rt
    final = final_g is not None
    in_specs = [
        pl.BlockSpec((rt, d), lambda i: (i, 0)),
        pl.BlockSpec((rt, LANES), lambda i: (i, 0)),
        rows.mp_spec(5, d), rows.ms_spec(5, d),
    ] + [pl.BlockSpec((rt, d), functools.partial(lambda k, i: (i, k), k)) for k in range(TOP_K)]
    args = [x, gw, mod_p, mod_s] + [yk] * TOP_K
    if final:
        in_specs.append(pl.BlockSpec((1, d), lambda i: (0, 0)))
        args.append(final_g.reshape(1, d))
    return pl.pallas_call(
        functools.partial(_combine_kernel, rows, final),
        out_shape=jax.ShapeDtypeStruct((m, d), F32),
        grid=(rows.n_tiles,),
        in_specs=in_specs,
        out_specs=pl.BlockSpec((rt, d), lambda i: (i, 0)),
        compiler_params=_cparams("arbitrary"),
        name="moe_combine",
    )(*args)


def _moe_layer(rows, x, g2, mod_p, mod_s, router_w, router_b, w_gu, b_gu, w_down, b_down, final_g):
    m, d = x.shape
    h, idx, gw, rank, cnt = _norm_route(rows, x, g2, mod_p, mod_s, router_w, router_b)
    bm = MOE_ROWS
    tk = m * TOP_K
    n_blocks = -(-(tk + N_EXPERTS * (bm - 1)) // bm)
    n_rows = n_blocks * bm
    counts = cnt[0, :N_EXPERTS].astype(jnp.int32)
    padded = (counts + bm - 1) // bm * bm
    pad_end = jnp.cumsum(padded)
    pad_start = pad_end - padded
    idx4, rank4 = idx[:, :TOP_K], rank[:, :TOP_K]
    dest = (pad_start[idx4] + rank4).reshape(tk)
    tok = jnp.repeat(jnp.arange(m, dtype=jnp.int32), TOP_K)
    row_tok = jnp.zeros((n_rows,), jnp.int32).at[dest].set(tok)
    block_e = jnp.minimum(jnp.searchsorted(pad_end, jnp.arange(n_blocks, dtype=jnp.int32) * bm, side="right"),
                          N_EXPERTS - 1).astype(jnp.int32)
    n_used = (pad_end[-1] // bm).astype(jnp.int32).reshape(1)
    x_sorted = jnp.take(h, row_tok, axis=0)
    de = w_gu.shape[2] // 2
    y = _moe_experts(x_sorted, block_e, n_used, w_gu, b_gu, w_down, b_down, min(512, de), min(1024, d))
    yk = jnp.take(y, dest, axis=0).reshape(m, TOP_K * d)
    return _combine(rows.halved(), x, gw, mod_p, mod_s, yk, final_g)


def kernel(x_prompt, x_sample, state_gla, c_prompt, c_sample, ada_w, ada_b, norm1_g, norm2_g, final_g,
           gla_w_in, gla_w_gate2, gla_b_gate, gla_onorm_g, gla_w_out,
           sgu_w_in, sgu_b_in, sgu_ln_g, sgu_ln_b, sgu_w_s, sgu_b_s, sgu_w_out, sgu_b_out,
           router_w, router_b, exp_w_gu, exp_b_gu, exp_w_down, exp_b_down):
    batch, seq, d = x_prompt.shape
    dec_batch, dec_seq, _ = x_sample.shape
    depth = ada_w.shape[0]
    m_p, m_s = batch * seq, dec_batch * dec_seq
    m = m_p + m_s
    rt = min(ROW_TILE, m_s)
    rows = _Rows(batch, seq, m_s, rt)
    kd = d // 2

    n_c = batch + dec_batch
    n_c_pad = -(-n_c // SUBLANES) * SUBLANES
    c_all = jnp.pad(jnp.concatenate([c_prompt, c_sample], axis=0), ((0, n_c_pad - n_c), (0, 0)))
    mod_all = _adaln_all(c_all, ada_w, ada_b, min(1024, d))

    x = jnp.concatenate([x_prompt.reshape(m_p, d), x_sample.reshape(m_s, d)], axis=0)
    s_out_sample = None
    s_out_prompt, v_out_sample = [], []
    gla_nb = SUBLANES // dec_seq
    gla_ts = min(256, seq)

    for i in range(depth):
        mod_p = mod_all[i, :batch].reshape(batch, 6, 1, d)
        mod_s = jnp.repeat(mod_all[i, batch:n_c], dec_seq, axis=0).reshape(m_s, 6, d).transpose(1, 0, 2)
        h = _norm_mod(rows, x, norm1_g[i], mod_p, mod_s, 0, 1)
        j = i // 2
        if i % 2 == 0:
            w_in = gla_w_in[j]
            z = _matmul(rows, h, w_in[:, :2 * kd + 2 * d], name="gla_in")
            a = _matmul(rows, h, jnp.pad(w_in[:, 2 * kd + 2 * d:], ((0, 0), (0, LANES - GLA_GATE_RANK))),
                        name="gla_gate_in")
            w2p = jnp.pad(gla_w_gate2[j], ((0, LANES - GLA_GATE_RANK), (0, 0)))
            og, s_p = _gla_prompt(z, a, w2p, gla_b_gate[j], gla_onorm_g[j], batch, seq, m, gla_ts)
            og, s_out_sample = _gla_sample(z, a, w2p, gla_b_gate[j], gla_onorm_g[j], state_gla, j, og,
                                           s_out_sample, m_p, dec_batch, dec_seq, gla_nb)
            s_out_prompt.append(s_p)
            x = _matmul(rows, og, gla_w_out[j], epilogue="resid", resid=x, mod_p=mod_p, mod_s=mod_s,
                        which_gate=2, name="gla_out")
        else:
            zs = _matmul(rows, h, sgu_w_in[j], bias=sgu_b_in[j], epilogue="gelu", name="sgu_in")
            wmix, bmix = _sgu_mix_tables(sgu_w_s[j], sgu_b_s[j], dec_seq)
            biasfull = jnp.repeat(bmix, d // SGU_GROUPS, axis=2)
            us, vn_s = _sgu_spatial(zs, sgu_ln_g[j], sgu_ln_b[j], wmix, biasfull, m_p, m_s)
            v_out_sample.append(vn_s.reshape(dec_batch, dec_seq, d))
            x = _matmul(rows, us, sgu_w_out[j], bias=sgu_b_out[j], epilogue="resid", resid=x, mod_p=mod_p,
                        mod_s=mod_s, which_gate=2, name="sgu_out")
        x = _moe_layer(rows, x, norm2_g[i], mod_p, mod_s, router_w[i], router_b[i], exp_w_gu[i], exp_b_gu[i],
                       exp_w_down[i], exp_b_down[i], final_g if i == depth - 1 else None)

    y_prompt = x[:m_p].reshape(batch, seq, d)
    y_sample = x[m_p:].reshape(dec_batch, dec_seq, d)
    return (y_prompt, y_sample, jnp.stack(s_out_prompt), s_out_sample, jnp.stack(v_out_sample))
```

```python
import functools

import jax
import jax.numpy as jnp
from jax import lax
from jax.experimental import pallas as pl
from jax.experimental.pallas import tpu as pltpu

F32 = jnp.float32
BF16 = jnp.bfloat16
U32 = jnp.uint32

GLA_HEADS = 4
GLA_GATE_RANK = 16
GLA_GATE_NORMALIZER = 16.0
GLA_CHUNK = 32
SGU_GROUPS = 8
SGU_CHUNK = 128
N_EXPERTS = 32
TOP_K = 4
SWIGLU_LIMIT = 7.0
SWIGLU_ALPHA = 1.702
NORM_EPS = 1e-6

LANES = 128
SUBLANES = 8
VMEM_LIMIT = 56 * 1024 * 1024
ROW_TILE = 512
MOE_ROWS = 256
HI = lax.Precision.HIGHEST


def _cparams(*sem):
    return pltpu.CompilerParams(dimension_semantics=sem, vmem_limit_bytes=VMEM_LIMIT)


def _tile(n, target):
    t = min(n, target) // LANES * LANES
    while n % t:
        t -= LANES
    return t


def _bdot(a, b):
    return jnp.dot(a.astype(BF16), b.astype(BF16), preferred_element_type=F32)


def _silu(x):
    return x * (1.0 / (1.0 + jnp.exp(-x)))


def _pack_bf16_pairs(x):
    half = x.shape[1] // 2
    bits = pltpu.bitcast(x.astype(BF16).astype(F32), U32)
    return pltpu.bitcast((bits[:, :half] >> 16) | bits[:, half:], F32)


def _unpack_bf16_pairs(p):
    p = pltpu.bitcast(p, U32)
    return pltpu.bitcast(p << 16, F32), pltpu.bitcast(p & jnp.uint32(0xFFFF0000), F32)


def _adaln_kernel(c_ref, w_ref, b_ref, o_ref):
    o_ref[...] = _bdot(_silu(c_ref[...]), w_ref[...]) + b_ref[...]


def _adaln_all(c_all, ada_w, ada_b, bn):
    depth, d, n6 = ada_w.shape
    rows = c_all.shape[0]
    return pl.pallas_call(
        _adaln_kernel,
        out_shape=jax.ShapeDtypeStruct((depth, rows, n6), F32),
        grid=(depth, n6 // bn),
        in_specs=[
            pl.BlockSpec((rows, d), lambda l, n: (0, 0)),
            pl.BlockSpec((None, d, bn), lambda l, n: (l, 0, n)),
            pl.BlockSpec((None, 1, bn), lambda l, n: (l, 0, n)),
        ],
        out_specs=pl.BlockSpec((None, rows, bn), lambda l, n: (l, 0, n)),
        compiler_params=_cparams("arbitrary", "arbitrary"),
        name="adaln",
    )(c_all, ada_w, ada_b.reshape(depth, 1, n6))


class _Rows:
    def __init__(self, batch, seq, m_sample, d, rt):
        self.rt = rt
        self.d = d
        self.tiles_per_batch = seq // rt
        self.n_prompt_tiles = batch * seq // rt
        self.n_tiles = self.n_prompt_tiles + m_sample // rt
        self.batch = batch
        self.halved = lambda: _Rows(batch, seq, m_sample, d, rt // 2)

    def mp_spec(self, which, width, col=False):
        tpb, nb = self.tiles_per_batch, self.batch
        if not col:
            return pl.BlockSpec((None, None, 1, width), lambda i: (jnp.minimum(i // tpb, nb - 1), which, 0, 0))
        return pl.BlockSpec((None, None, 1, width),
                            lambda n, i: (jnp.minimum(i // tpb, nb - 1), which, 0, n))

    def ms_spec(self, which, width, col=False):
        npt = self.n_prompt_tiles
        c0 = which * (self.d // width)
        if not col:
            return pl.BlockSpec((self.rt, width), lambda i: (jnp.maximum(i - npt, 0), c0))
        return pl.BlockSpec((self.rt, width), lambda n, i: (jnp.maximum(i - npt, 0), c0 + n))


def _pick_mod(i, rows, mp_ref, ms_ref):
    return jnp.where(i >= rows.n_prompt_tiles, ms_ref[...], mp_ref[...])


def _rms(x, g):
    return x * lax.rsqrt(jnp.mean(x * x, axis=-1, keepdims=True) + NORM_EPS) * g


def _norm_mod_kernel(rows, x_ref, g_ref, mp_sh, mp_sc, ms_sh, ms_sc, h_ref):
    i = pl.program_id(0)
    y = _rms(x_ref[...], g_ref[...])
    h = y * (1.0 + _pick_mod(i, rows, mp_sc, ms_sc)) + _pick_mod(i, rows, mp_sh, ms_sh)
    h_ref[...] = h.astype(h_ref.dtype)


def _norm_mod(rows, x, g, mod_p, mod_s, which_shift, which_scale):
    m, d = x.shape
    rt = rows.rt
    return pl.pallas_call(
        functools.partial(_norm_mod_kernel, rows),
        out_shape=jax.ShapeDtypeStruct((m, d), BF16),
        grid=(rows.n_tiles,),
        in_specs=[
            pl.BlockSpec((rt, d), lambda i: (i, 0)),
            pl.BlockSpec((1, d), lambda i: (0, 0)),
            rows.mp_spec(which_shift, d), rows.mp_spec(which_scale, d),
            rows.ms_spec(which_shift, d), rows.ms_spec(which_scale, d),
        ],
        out_specs=pl.BlockSpec((rt, d), lambda i: (i, 0)),
        compiler_params=_cparams("arbitrary"),
        name="norm_mod",
    )(x, g.reshape(1, d), mod_p, mod_p, mod_s, mod_s)


def _lane_pick(lane, k, col, acc):
    return jnp.where(lane == k, col, acc)


def _norm_route_kernel(rows, x_ref, g_ref, mp_sh, mp_sc, ms_sh, ms_sc, rw_ref, rb_ref,
                       h_ref, idx_ref, gw_ref, rank_ref, cnt_ref, carry_ref):
    i = pl.program_id(0)
    rt = rows.rt

    @pl.when(i == 0)
    def _():
        carry_ref[...] = jnp.zeros_like(carry_ref)

    y = _rms(x_ref[...], g_ref[...])
    h = y * (1.0 + _pick_mod(i, rows, mp_sc, ms_sc)) + _pick_mod(i, rows, mp_sh, ms_sh)
    h_ref[...] = _pack_bf16_pairs(h)

    logits = jnp.dot(h, rw_ref[...], preferred_element_type=F32, precision=HI) + rb_ref[...]
    lane = lax.broadcasted_iota(jnp.int32, (rt, LANES), 1)
    neg = jnp.float32(-jnp.inf)
    work = jnp.where(lane < N_EXPERTS, logits, neg)
    member = jnp.zeros((rt, LANES), F32)
    vals, idxs = [], []
    for _ in range(TOP_K):
        mx = jnp.max(work, axis=-1, keepdims=True)
        idx = jnp.min(jnp.where(work == mx, lane, LANES), axis=-1, keepdims=True)
        sel = lane == idx
        member = jnp.where(sel, 1.0, member)
        work = jnp.where(sel, neg, work)
        vals.append(mx)
        idxs.append(idx)
    exps = [jnp.exp(v - vals[0]) for v in vals]
    denom = exps[0]
    for e in exps[1:]:
        denom = denom + e

    r_i = lax.broadcasted_iota(jnp.int32, (rt, rt), 0)
    c_i = lax.broadcasted_iota(jnp.int32, (rt, rt), 1)
    below = jnp.where(r_i > c_i, 1.0, 0.0).astype(BF16)
    before = jnp.dot(below, member.astype(BF16), preferred_element_type=F32) + carry_ref[0:1, :]

    idx_out = jnp.zeros((rt, LANES), jnp.int32)
    gw_out = jnp.zeros((rt, LANES), F32)
    rank_out = jnp.zeros((rt, LANES), jnp.int32)
    for k in range(TOP_K):
        rank_k = jnp.sum(jnp.where(lane == idxs[k], before, 0.0), axis=-1, keepdims=True)
        idx_out = _lane_pick(lane, k, idxs[k], idx_out)
        gw_out = _lane_pick(lane, k, exps[k] / denom, gw_out)
        rank_out = _lane_pick(lane, k, rank_k.astype(jnp.int32), rank_out)
    idx_ref[...] = idx_out
    gw_ref[...] = gw_out
    rank_ref[...] = rank_out

    carry_ref[...] = carry_ref[...] + jnp.sum(member, axis=0, keepdims=True)
    cnt_ref[...] = carry_ref[...]


def _norm_route(rows, x, g, mod_p, mod_s, router_w, router_b):
    m, d = x.shape
    rt = rows.rt
    rw = jnp.pad(router_w, ((0, 0), (0, LANES - N_EXPERTS)))
    rb = jnp.pad(router_b, (0, LANES - N_EXPERTS)).reshape(1, LANES)
    row_spec = pl.BlockSpec((rt, LANES), lambda i: (i, 0))
    return pl.pallas_call(
        functools.partial(_norm_route_kernel, rows),
        out_shape=(
            jax.ShapeDtypeStruct((m, d // 2), F32),
            jax.ShapeDtypeStruct((m, LANES), jnp.int32),
            jax.ShapeDtypeStruct((m, LANES), F32),
            jax.ShapeDtypeStruct((m, LANES), jnp.int32),
            jax.ShapeDtypeStruct((SUBLANES, LANES), F32),
        ),
        grid=(rows.n_tiles,),
        in_specs=[
            pl.BlockSpec((rt, d), lambda i: (i, 0)),
            pl.BlockSpec((1, d), lambda i: (0, 0)),
            rows.mp_spec(3, d), rows.mp_spec(4, d),
            rows.ms_spec(3, d), rows.ms_spec(4, d),
            pl.BlockSpec((d, LANES), lambda i: (0, 0)),
            pl.BlockSpec((1, LANES), lambda i: (0, 0)),
        ],
        out_specs=(
            pl.BlockSpec((rt, d // 2), lambda i: (i, 0)),
            row_spec, row_spec, row_spec,
            pl.BlockSpec((SUBLANES, LANES), lambda i: (0, 0)),
        ),
        scratch_shapes=[pltpu.VMEM((SUBLANES, LANES), F32)],
        compiler_params=_cparams("arbitrary"),
        name="norm_route",
    )(x, g.reshape(1, d), mod_p, mod_p, mod_s, mod_s, rw, rb)


def _gelu(x):
    return 0.5 * x * (1.0 + lax.erf(x * (2.0 ** -0.5)))


def _mm_kernel(rows, epilogue, has_bias, *refs):
    refs = list(refs)
    x_ref, w_ref = refs[0], refs[1]
    pos = 2
    b_ref = None
    if has_bias:
        b_ref = refs[pos]
        pos += 1
    if epilogue == "resid":
        res_ref, mp_g, ms_g = refs[pos:pos + 3]
        pos += 3
    o_ref, wbf_ref = refs[pos], refs[pos + 1]
    i = pl.program_id(1)

    @pl.when(i == 0)
    def _():
        wbf_ref[...] = w_ref[...].astype(BF16)

    acc = jnp.dot(x_ref[...].astype(BF16), wbf_ref[...], preferred_element_type=F32)
    if has_bias:
        acc = acc + b_ref[...]
    if epilogue == "gelu":
        acc = _gelu(acc)
    elif epilogue == "resid":
        acc = res_ref[...] + _pick_mod(i, rows, mp_g, ms_g) * acc
    o_ref[...] = acc.astype(o_ref.dtype)


def _matmul(rows, x, w_all, layer, n, bias=None, epilogue="plain", resid=None, mod_p=None, mod_s=None,
            which_gate=None, bn=1024, out_dtype=F32, name="matmul"):
    m, k = x.shape
    bn = _tile(n, bn)
    rt = rows.rt
    in_specs = [pl.BlockSpec((rt, k), lambda j, i: (i, 0)),
                pl.BlockSpec((None, k, bn), lambda j, i: (layer, 0, j))]
    args = [x, w_all]
    if bias is not None:
        in_specs.append(pl.BlockSpec((1, bn), lambda j, i: (0, j)))
        args.append(bias.reshape(1, n))
    if epilogue == "resid":
        in_specs += [pl.BlockSpec((rt, bn), lambda j, i: (i, j)),
                     rows.mp_spec(which_gate, bn, col=True), rows.ms_spec(which_gate, bn, col=True)]
        args += [resid, mod_p, mod_s]
    return pl.pallas_call(
        functools.partial(_mm_kernel, rows, epilogue, bias is not None),
        out_shape=jax.ShapeDtypeStruct((m, n), out_dtype),
        grid=(n // bn, m // rt),
        in_specs=in_specs,
        out_specs=pl.BlockSpec((rt, bn), lambda j, i: (i, j)),
        scratch_shapes=[pltpu.VMEM((k, bn), BF16)],
        compiler_params=_cparams("arbitrary", "arbitrary"),
        name=name,
    )(*args)


def _log_sigmoid(x):
    return jnp.minimum(x, 0.0) - jnp.log1p(jnp.exp(-jnp.abs(x)))


def _gla_gate_logs(a, w2, b):
    return _log_sigmoid(_bdot(a, w2) + b) * (1.0 / GLA_GATE_NORMALIZER)


def _split3(x):
    hi = x.astype(BF16)
    r1 = x - hi.astype(F32)
    mid = r1.astype(BF16)
    lo = (r1 - mid.astype(F32)).astype(BF16)
    return hi, mid, lo


def _gla_cumsums(lg):
    c = lg.shape[0]
    r_i = lax.broadcasted_iota(jnp.int32, (c, c), 0)
    c_i = lax.broadcasted_iota(jnp.int32, (c, c), 1)
    tri = jnp.where(r_i >= c_i, 1.0, 0.0).astype(BF16)
    ones = jnp.ones((c, LANES), BF16)
    b = None
    tot = None
    for part in _split3(lg):
        pb = jnp.dot(tri, part, preferred_element_type=F32)
        pt = lax.dot_general(part, ones, (((0,), (0,)), ((), ())), preferred_element_type=F32)
        b = pb if b is None else b + pb
        tot = pt if tot is None else tot + pt
    return b, tot


def _gla_chunk(q, k, v, b, b_end_col, s, scale):
    c, dk = q.shape
    dv = v.shape[1]
    r_i = lax.broadcasted_iota(jnp.int32, (c, c), 0)
    c_i = lax.broadcasted_iota(jnp.int32, (c, c), 1)
    causal = r_i >= c_i
    b_end_row = b[c - 1:c, :]
    q_dec = (q * scale) * jnp.exp(b)
    k_inv = k * jnp.exp(-b)
    k_end = k * jnp.exp(b_end_row - b)
    att = lax.dot_general(q_dec.astype(BF16), k_inv.astype(BF16), (((1,), (1,)), ((), ())),
                          preferred_element_type=F32)
    att = jnp.where(causal, att, 0.0)
    o = _bdot(att, v) + _bdot(q_dec, s)
    decay = jnp.exp(b_end_col)
    decay = jnp.concatenate([decay] * (dv // LANES), axis=1)
    kv = lax.dot_general(k_end.astype(BF16), v.astype(BF16), (((0,), (0,)), ((), ())),
                         preferred_element_type=F32)
    return o, s * decay + kv


def _gla_out(o, g, onorm_g):
    return _rms(o, onorm_g) * _silu(g)


def _gla_prompt_kernel(dims, q_ref, k_ref, v_ref, g_ref, a_ref, w2_ref, bg_ref, on_ref,
                       og_ref, sout_ref, s_ref, lg_ref):
    heads, dk, dv, chunk, ts = dims
    step = pl.program_id(1)

    @pl.when(step == 0)
    def _():
        s_ref[...] = jnp.zeros_like(s_ref)

    lg_ref[...] = _gla_gate_logs(a_ref[...], w2_ref[...], bg_ref[...])
    scale = dk ** -0.5

    def body(ci, carry):
        r0 = pl.multiple_of(ci * chunk, chunk)
        rsl = pl.ds(r0, chunk)
        b_all, tot_all = _gla_cumsums(lg_ref[rsl, :])
        for h in range(heads):
            ksl = slice(h * dk, (h + 1) * dk)
            vsl = slice(h * dv, (h + 1) * dv)
            o, s_new = _gla_chunk(q_ref[rsl, ksl], k_ref[rsl, ksl], v_ref[rsl, vsl], b_all[:, ksl],
                                  tot_all[ksl, :], s_ref[h], scale)
            s_ref[h] = s_new
            og_ref[rsl, vsl] = _gla_out(o, g_ref[rsl, vsl], on_ref[...]).astype(og_ref.dtype)
        return carry

    lax.fori_loop(0, ts // chunk, body, 0)

    @pl.when(step == pl.num_programs(1) - 1)
    def _():
        sout_ref[...] = s_ref[...]


def _gla_prompt(z, a, w2p, b_gate, onorm_g, batch, seq, m_total, ts):
    d2 = z.shape[1]
    d = d2 // 3
    kd = d // 2
    heads = GLA_HEADS
    dk, dv = kd // heads, d // heads
    spb = seq // ts
    dims = (heads, dk, dv, GLA_CHUNK, ts)
    row = lambda n, s: n * spb + s
    return pl.pallas_call(
        functools.partial(_gla_prompt_kernel, dims),
        out_shape=(jax.ShapeDtypeStruct((m_total, d), F32),
                   jax.ShapeDtypeStruct((batch, heads, dk, dv), F32)),
        grid=(batch, spb),
        in_specs=[
            pl.BlockSpec((ts, kd), lambda n, s: (row(n, s), 0)),
            pl.BlockSpec((ts, kd), lambda n, s: (row(n, s), 1)),
            pl.BlockSpec((ts, d), lambda n, s: (row(n, s), 1)),
            pl.BlockSpec((ts, d), lambda n, s: (row(n, s), 2)),
            pl.BlockSpec((ts, LANES), lambda n, s: (row(n, s), 0)),
            pl.BlockSpec((LANES, kd), lambda n, s: (0, 0)),
            pl.BlockSpec((1, kd), lambda n, s: (0, 0)),
            pl.BlockSpec((1, dv), lambda n, s: (0, 0)),
        ],
        out_specs=(pl.BlockSpec((ts, d), lambda n, s: (row(n, s), 0)),
                   pl.BlockSpec((None, heads, dk, dv), lambda n, s: (n, 0, 0, 0))),
        scratch_shapes=[pltpu.VMEM((heads, dk, dv), F32), pltpu.VMEM((ts, kd), F32)],
        compiler_params=_cparams("arbitrary", "arbitrary"),
        name="gla_prompt",
    )(z, z, z, z, a, w2p, b_gate.reshape(1, kd), onorm_g.reshape(1, dv))


def _gla_sample_kernel(dims, n_alias, q_ref, k_ref, v_ref, g_ref, a_ref, w2_ref, bg_ref, on_ref, s0_ref, *refs):
    og_ref, sout_ref = refs[n_alias:]
    heads, dk, dv, nb, dec = dims
    c = nb * dec
    scale = dk ** -0.5
    lg_all = _gla_gate_logs(a_ref[...], w2_ref[...], bg_ref[...])
    row = lax.broadcasted_iota(jnp.int32, (c, 1), 0)
    mine = [(row >= b * dec) & (row < (b + 1) * dec) for b in range(nb)]
    sums = [_gla_cumsums(jnp.where(mine[b], lg_all, 0.0)) for b in range(nb)]
    for h in range(heads):
        ksl = slice(h * dk, (h + 1) * dk)
        vsl = slice(h * dv, (h + 1) * dv)
        q, k, v = q_ref[:, ksl], k_ref[:, ksl], v_ref[:, vsl]
        o_all = jnp.zeros((c, dv), F32)
        for b in range(nb):
            z = lambda t: jnp.where(mine[b], t, 0.0)
            o, s_new = _gla_chunk(z(q), z(k), z(v), sums[b][0][:, ksl], sums[b][1][ksl, :], s0_ref[b, h], scale)
            sout_ref[b, h] = s_new
            o_all = jnp.where(mine[b], o, o_all)
        og_ref[:, vsl] = _gla_out(o_all, g_ref[:, vsl], on_ref[...]).astype(og_ref.dtype)


def _gla_sample(z, a, w2p, b_gate, onorm_g, state_all, j, og, s_prev, m_prompt, dec_batch, dec_seq, nb):
    d2 = z.shape[1]
    d = d2 // 3
    kd = d // 2
    heads = GLA_HEADS
    dk, dv = kd // heads, d // heads
    c = nb * dec_seq
    r0 = m_prompt // c
    dims = (heads, dk, dv, nb, dec_seq)
    st_spec = pl.BlockSpec((None, nb, heads, dk, dv), lambda i: (j, i, 0, 0, 0))
    in_specs = [
        pl.BlockSpec((c, kd), lambda i: (r0 + i, 0)),
        pl.BlockSpec((c, kd), lambda i: (r0 + i, 1)),
        pl.BlockSpec((c, d), lambda i: (r0 + i, 1)),
        pl.BlockSpec((c, d), lambda i: (r0 + i, 2)),
        pl.BlockSpec((c, LANES), lambda i: (r0 + i, 0)),
        pl.BlockSpec((LANES, kd), lambda i: (0, 0)),
        pl.BlockSpec((1, kd), lambda i: (0, 0)),
        pl.BlockSpec((1, dv), lambda i: (0, 0)),
        st_spec,
        pl.BlockSpec(memory_space=pl.ANY),
    ]
    args = [z, z, z, z, a, w2p, b_gate.reshape(1, kd), onorm_g.reshape(1, dv), state_all, og]
    aliases = {len(args) - 1: 0}
    if s_prev is not None:
        in_specs.append(pl.BlockSpec(memory_space=pl.ANY))
        args.append(s_prev)
        aliases[len(args) - 1] = 1
    return pl.pallas_call(
        functools.partial(_gla_sample_kernel, dims, len(aliases)),
        out_shape=(jax.ShapeDtypeStruct(og.shape, og.dtype),
                   jax.ShapeDtypeStruct(state_all.shape, state_all.dtype)),
        grid=(dec_batch // nb,),
        in_specs=in_specs,
        out_specs=(pl.BlockSpec((c, d), lambda i: (r0 + i, 0)), st_spec),
        input_output_aliases=aliases,
        compiler_params=_cparams("arbitrary"),
        name="gla_sample",
    )(*args)


def _sgu_kernel(groups, u_ref, v_ref, lg_ref, lb_ref, w_ref, bias_ref, us_ref, vn_ref):
    v = v_ref[...]
    mu = jnp.mean(v, axis=-1, keepdims=True)
    vc = v - mu
    vn = vc * lax.rsqrt(jnp.mean(vc * vc, axis=-1, keepdims=True) + NORM_EPS) * lg_ref[...] + lb_ref[...]
    vn_ref[...] = vn
    gd = v.shape[1] // groups
    for g in range(groups):
        sl = slice(g * gd, (g + 1) * gd)
        s = _bdot(w_ref[g], vn[:, sl]) + bias_ref[:, sl]
        us_ref[:, sl] = (u_ref[:, sl] * s).astype(us_ref.dtype)


def _sgu_spatial(zs, ln_g, ln_b, wmix, biasfull, m_prompt, m_sample):
    m, d2 = zs.shape
    sd = d2 // 2
    ch = SGU_CHUNK
    npc = m_prompt // ch
    which = lambda c: jnp.where(c >= npc, 1, 0)
    return pl.pallas_call(
        functools.partial(_sgu_kernel, SGU_GROUPS),
        out_shape=(jax.ShapeDtypeStruct((m, sd), BF16), jax.ShapeDtypeStruct((m_sample, sd), F32)),
        grid=(m // ch,),
        in_specs=[
            pl.BlockSpec((ch, sd), lambda c: (c, 0)),
            pl.BlockSpec((ch, sd), lambda c: (c, 1)),
            pl.BlockSpec((1, sd), lambda c: (0, 0)),
            pl.BlockSpec((1, sd), lambda c: (0, 0)),
            pl.BlockSpec((None, SGU_GROUPS, ch, ch), lambda c: (which(c), 0, 0, 0)),
            pl.BlockSpec((None, ch, sd), lambda c: (which(c), 0, 0)),
        ],
        out_specs=(pl.BlockSpec((ch, sd), lambda c: (c, 0)),
                   pl.BlockSpec((ch, sd), lambda c: (jnp.maximum(c - npc, 0), 0))),
        compiler_params=_cparams("arbitrary"),
        name="sgu_spatial",
    )(zs, zs, ln_g.reshape(1, sd), ln_b.reshape(1, sd), wmix, biasfull)


def _sgu_mix_tables(w_s, b_s, dec_seq):
    groups, ch, _ = w_s.shape
    tril = jnp.tril(jnp.ones((ch, ch), bool))
    wp = jnp.where(tril, w_s, 0.0)
    small = jnp.where(jnp.tril(jnp.ones((dec_seq, dec_seq), bool)), w_s[:, :dec_seq, :dec_seq], 0.0)
    reps = ch // dec_seq
    eye = jnp.eye(reps, dtype=w_s.dtype)
    ws = jnp.einsum("ab,gts->gatbs", eye, small).reshape(groups, ch, ch)
    bp = b_s.T
    bs = jnp.tile(b_s[:, :dec_seq].T, (reps, 1))
    return jnp.stack([wp, ws]), jnp.stack([bp, bs])


def _expert_changed(be_ref, b):
    prev = be_ref[jnp.maximum(b - 1, 0)]
    return jnp.logical_or(b == 0, be_ref[b] != prev)


def _moe_gu_kernel(be_ref, nu_ref, x_ref, wg_ref, wu_ref, bg_ref, bu_ref, hid_ref, wg_bf, wu_bf):
    b = pl.program_id(1)
    live = b < nu_ref[0]

    @pl.when(jnp.logical_and(_expert_changed(be_ref, b), live))
    def _():
        wg_bf[...] = wg_ref[...].astype(BF16)
        wu_bf[...] = wu_ref[...].astype(BF16)

    @pl.when(live)
    def _():
        lo, hi = _unpack_bf16_pairs(x_ref[...])
        x = jnp.concatenate([lo, hi], axis=1).astype(BF16)
        gate = jnp.dot(x, wg_bf[...], preferred_element_type=F32) + bg_ref[...]
        up = jnp.dot(x, wu_bf[...], preferred_element_type=F32) + bu_ref[...]
        gate = jnp.minimum(gate, SWIGLU_LIMIT)
        up = jnp.clip(up, -SWIGLU_LIMIT, SWIGLU_LIMIT)
        glu = gate * (1.0 / (1.0 + jnp.exp(-gate * SWIGLU_ALPHA)))
        hid_ref[...] = ((up + 1.0) * glu).astype(hid_ref.dtype)

    @pl.when(jnp.logical_not(live))
    def _():
        hid_ref[...] = jnp.zeros_like(hid_ref)


def _moe_down_kernel(be_ref, nu_ref, h_ref, w_ref, b_ref, y_ref, w_bf):
    b = pl.program_id(1)
    live = b < nu_ref[0]

    @pl.when(jnp.logical_and(_expert_changed(be_ref, b), live))
    def _():
        w_bf[...] = w_ref[...].astype(BF16)

    @pl.when(live)
    def _():
        y_ref[...] = jnp.dot(h_ref[...], w_bf[...], preferred_element_type=F32) + b_ref[...]

    @pl.when(jnp.logical_not(live))
    def _():
        y_ref[...] = jnp.zeros_like(y_ref)


def _moe_experts(x_sorted, block_e, n_used, layer, w_gu, b_gu, w_down, b_down, bnh, bnd):
    n_rows, dh = x_sorted.shape
    d = 2 * dh
    depth, n_exp, _, de2 = w_gu.shape
    de = de2 // 2
    bm = MOE_ROWS
    n_blocks = n_rows // bm
    nj = de // bnh
    b_gu4 = b_gu.reshape(depth, n_exp, 1, de2)
    hid = pl.pallas_call(
        _moe_gu_kernel,
        out_shape=jax.ShapeDtypeStruct((n_rows, de), BF16),
        grid_spec=pltpu.PrefetchScalarGridSpec(
            num_scalar_prefetch=2,
            grid=(nj, n_blocks),
            in_specs=[
                pl.BlockSpec((bm, dh), lambda j, b, be, nu: (b, 0)),
                pl.BlockSpec((None, None, d, bnh), lambda j, b, be, nu: (layer, be[b], 0, j)),
                pl.BlockSpec((None, None, d, bnh), lambda j, b, be, nu: (layer, be[b], 0, nj + j)),
                pl.BlockSpec((None, None, 1, bnh), lambda j, b, be, nu: (layer, be[b], 0, j)),
                pl.BlockSpec((None, None, 1, bnh), lambda j, b, be, nu: (layer, be[b], 0, nj + j)),
            ],
            out_specs=pl.BlockSpec((bm, bnh), lambda j, b, be, nu: (b, j)),
            scratch_shapes=[pltpu.VMEM((d, bnh), BF16), pltpu.VMEM((d, bnh), BF16)],
        ),
        compiler_params=_cparams("arbitrary", "arbitrary"),
        name="moe_gate_up",
    )(block_e, n_used, x_sorted, w_gu, w_gu, b_gu4, b_gu4)

    njd = d // bnd
    return pl.pallas_call(
        _moe_down_kernel,
        out_shape=jax.ShapeDtypeStruct((n_rows, d), F32),
        grid_spec=pltpu.PrefetchScalarGridSpec(
            num_scalar_prefetch=2,
            grid=(njd, n_blocks),
            in_specs=[
                pl.BlockSpec((bm, de), lambda j, b, be, nu: (b, 0)),
                pl.BlockSpec((None, None, de, bnd), lambda j, b, be, nu: (layer, be[b], 0, j)),
                pl.BlockSpec((None, None, 1, bnd), lambda j, b, be, nu: (layer, be[b], 0, j)),
            ],
            out_specs=pl.BlockSpec((bm, bnd), lambda j, b, be, nu: (b, j)),
            scratch_shapes=[pltpu.VMEM((de, bnd), BF16)],
        ),
        compiler_params=_cparams("arbitrary", "arbitrary"),
        name="moe_down",
    )(block_e, n_used, hid, w_down, b_down.reshape(depth, n_exp, 1, d))


def _combine_kernel(rows, final, *refs):
    x_ref, gw_ref, mp_g, ms_g = refs[0], refs[1], refs[2], refs[3]
    y_refs = refs[4:4 + TOP_K]
    pos = 4 + TOP_K
    if final:
        fg_ref, o_ref = refs[pos], refs[pos + 1]
    else:
        o_ref = refs[pos]
    i = pl.program_id(0)
    gw = gw_ref[...]
    f = y_refs[0][...] * gw[:, 0:1]
    for k in range(1, TOP_K):
        f = f + y_refs[k][...] * gw[:, k:k + 1]
    x_new = x_ref[...] + _pick_mod(i, rows, mp_g, ms_g) * f
    if final:
        x_new = _rms(x_new, fg_ref[...])
    o_ref[...] = x_new


def _combine(rows, x, gw, mod_p, mod_s, yk, final_g=None):
    m, d = x.shape
    rt = rows.rt
    nt = rows.n_tiles
    final = final_g is not None
    in_specs = [
        pl.BlockSpec((rt, d), lambda i: (i, 0)),
        pl.BlockSpec((rt, LANES), lambda i: (i, 0)),
        rows.mp_spec(5, d), rows.ms_spec(5, d),
    ] + [pl.BlockSpec((rt, d), functools.partial(lambda k, i: (k * nt + i, 0), k)) for k in range(TOP_K)]
    args = [x, gw, mod_p, mod_s] + [yk] * TOP_K
    if final:
        in_specs.append(pl.BlockSpec((1, d), lambda i: (0, 0)))
        args.append(final_g.reshape(1, d))
    return pl.pallas_call(
        functools.partial(_combine_kernel, rows, final),
        out_shape=jax.ShapeDtypeStruct((m, d), F32),
        grid=(nt,),
        in_specs=in_specs,
        out_specs=pl.BlockSpec((rt, d), lambda i: (i, 0)),
        compiler_params=_cparams("arbitrary"),
        name="moe_combine",
    )(*args)


def _moe_layer(rows, x, g2, mod_p, mod_s, layer, router_w, router_b, w_gu, b_gu, w_down, b_down, final_g):
    m, d = x.shape
    h, idx, gw, rank, cnt = _norm_route(rows, x, g2, mod_p, mod_s, router_w[layer], router_b[layer])
    bm = MOE_ROWS
    tk = m * TOP_K
    n_blocks = -(-(tk + N_EXPERTS * (bm - 1)) // bm)
    n_rows = n_blocks * bm
    counts = cnt[0, :N_EXPERTS].astype(jnp.int32)
    padded = (counts + bm - 1) // bm * bm
    pad_end = jnp.cumsum(padded)
    pad_start = pad_end - padded
    dest = (pad_start[idx[:, :TOP_K]] + rank[:, :TOP_K]).T.reshape(tk)
    tok = jnp.tile(jnp.arange(m, dtype=jnp.int32), TOP_K)
    row_tok = jnp.zeros((n_rows,), jnp.int32).at[dest].set(tok, unique_indices=True, indices_are_sorted=False)
    block_start = jnp.arange(n_blocks, dtype=jnp.int32) * bm
    block_e = jnp.minimum(jnp.sum(pad_end[None, :] <= block_start[:, None], axis=1), N_EXPERTS - 1).astype(jnp.int32)
    n_used = (pad_end[-1] // bm).astype(jnp.int32).reshape(1)
    x_sorted = jnp.take(h, row_tok, axis=0, mode="clip")
    de = w_gu.shape[3] // 2
    y = _moe_experts(x_sorted, block_e, n_used, layer, w_gu, b_gu, w_down, b_down, min(1024, de), min(2048, d))
    yk = jnp.take(y, dest, axis=0, mode="clip")
    return _combine(rows.halved(), x, gw, mod_p, mod_s, yk, final_g)


def kernel(x_prompt, x_sample, state_gla, c_prompt, c_sample, ada_w, ada_b, norm1_g, norm2_g, final_g,
           gla_w_in, gla_w_gate2, gla_b_gate, gla_onorm_g, gla_w_out,
           sgu_w_in, sgu_b_in, sgu_ln_g, sgu_ln_b, sgu_w_s, sgu_b_s, sgu_w_out, sgu_b_out,
           router_w, router_b, exp_w_gu, exp_b_gu, exp_w_down, exp_b_down):
    batch, seq, d = x_prompt.shape
    dec_batch, dec_seq, _ = x_sample.shape
    depth = ada_w.shape[0]
    m_p, m_s = batch * seq, dec_batch * dec_seq
    m = m_p + m_s
    rt = min(ROW_TILE, m_s)
    rows = _Rows(batch, seq, m_s, d, rt)
    kd = d // 2
    n_qkvg = 2 * kd + 2 * d

    n_c = batch + dec_batch
    n_c_pad = -(-n_c // SUBLANES) * SUBLANES
    c_all = jnp.pad(jnp.concatenate([c_prompt, c_sample], axis=0), ((0, n_c_pad - n_c), (0, 0)))
    mod_all = _adaln_all(c_all, ada_w, ada_b, min(1024, d))

    x = jnp.concatenate([x_prompt.reshape(m_p, d), x_sample.reshape(m_s, d)], axis=0)
    s_out_sample = None
    s_out_prompt, v_out_sample = [], []
    gla_nb = SUBLANES // dec_seq
    gla_ts = min(256, seq)

    for i in range(depth):
        mod_p = mod_all[i, :batch].reshape(batch, 6, 1, d)
        mod_s = jnp.repeat(mod_all[i, batch:n_c], dec_seq, axis=0)
        h = _norm_mod(rows, x, norm1_g[i], mod_p, mod_s, 0, 1)
        j = i // 2
        if i % 2 == 0:
            z = _matmul(rows, h, gla_w_in, j, n_qkvg, name="gla_in")
            w_a = jnp.pad(gla_w_in[j, :, n_qkvg:], ((0, 0), (0, LANES - GLA_GATE_RANK)))[None]
            a = _matmul(rows, h, w_a, 0, LANES, name="gla_gate_in")
            w2p = jnp.pad(gla_w_gate2[j], ((0, LANES - GLA_GATE_RANK), (0, 0)))
            og, s_p = _gla_prompt(z, a, w2p, gla_b_gate[j], gla_onorm_g[j], batch, seq, m, gla_ts)
            og, s_out_sample = _gla_sample(z, a, w2p, gla_b_gate[j], gla_onorm_g[j], state_gla, j, og,
                                           s_out_sample, m_p, dec_batch, dec_seq, gla_nb)
            s_out_prompt.append(s_p)
            x = _matmul(rows, og, gla_w_out, j, d, epilogue="resid", resid=x, mod_p=mod_p, mod_s=mod_s,
                        which_gate=2, name="gla_out")
        else:
            zs = _matmul(rows, h, sgu_w_in, j, 2 * d, bias=sgu_b_in[j], epilogue="gelu", name="sgu_in")
            wmix, bmix = _sgu_mix_tables(sgu_w_s[j], sgu_b_s[j], dec_seq)
            biasfull = jnp.repeat(bmix, d // SGU_GROUPS, axis=2)
            us, vn_s = _sgu_spatial(zs, sgu_ln_g[j], sgu_ln_b[j], wmix, biasfull, m_p, m_s)
            v_out_sample.append(vn_s.reshape(dec_batch, dec_seq, d))
            x = _matmul(rows, us, sgu_w_out, j, d, bias=sgu_b_out[j], epilogue="resid", resid=x, mod_p=mod_p,
                        mod_s=mod_s, which_gate=2, name="sgu_out")
        x = _moe_layer(rows, x, norm2_g[i], mod_p, mod_s, i, router_w, router_b, exp_w_gu, exp_b_gu,
                       exp_w_down, exp_b_down, final_g if i == depth - 1 else None)

    y_prompt = x[:m_p].reshape(batch, seq, d)
    y_sample = x[m_p:].reshape(dec_batch, dec_seq, d)
    return (y_prompt, y_sample, jnp.stack(s_out_prompt), s_out_sample, jnp.stack(v_out_sample))
```

```python
import functools

import jax
import jax.numpy as jnp
from jax import lax
from jax.experimental import pallas as pl
from jax.experimental.pallas import tpu as pltpu

F32 = jnp.float32
BF16 = jnp.bfloat16
U32 = jnp.uint32

GLA_HEADS = 4
GLA_GATE_RANK = 16
GLA_GATE_NORMALIZER = 16.0
GLA_CHUNK = 32
SGU_GROUPS = 8
SGU_CHUNK = 128
N_EXPERTS = 32
TOP_K = 4
SWIGLU_LIMIT = 7.0
SWIGLU_ALPHA = 1.702
NORM_EPS = 1e-6

LANES = 128
SUBLANES = 8
VMEM_LIMIT = 56 * 1024 * 1024
ROW_TILE = 512
MOE_ROWS = 256
HI = lax.Precision.HIGHEST


def _cparams(*sem):
    return pltpu.CompilerParams(dimension_semantics=sem, vmem_limit_bytes=VMEM_LIMIT)


def _tile(n, target):
    t = min(n, target) // LANES * LANES
    while n % t:
        t -= LANES
    return t


def _bdot(a, b):
    return jnp.dot(a.astype(BF16), b.astype(BF16), preferred_element_type=F32)


def _silu(x):
    return x * (1.0 / (1.0 + jnp.exp(-x)))


def _pack_bf16_pairs(x):
    half = x.shape[1] // 2
    bits = pltpu.bitcast(x.astype(BF16).astype(F32), U32)
    return pltpu.bitcast((bits[:, :half] >> 16) | bits[:, half:], F32)


def _unpack_bf16_pairs(p):
    p = pltpu.bitcast(p, U32)
    return pltpu.bitcast(p << 16, F32), pltpu.bitcast(p & jnp.uint32(0xFFFF0000), F32)


def _adaln_kernel(c_ref, w_ref, b_ref, o_ref):
    o_ref[...] = _bdot(_silu(c_ref[...]), w_ref[...]) + b_ref[...]


def _adaln_all(c_all, ada_w, ada_b, bn):
    depth, d, n6 = ada_w.shape
    rows = c_all.shape[0]
    return pl.pallas_call(
        _adaln_kernel,
        out_shape=jax.ShapeDtypeStruct((depth, rows, n6), F32),
        grid=(depth, n6 // bn),
        in_specs=[
            pl.BlockSpec((rows, d), lambda l, n: (0, 0)),
            pl.BlockSpec((None, d, bn), lambda l, n: (l, 0, n)),
            pl.BlockSpec((None, 1, bn), lambda l, n: (l, 0, n)),
        ],
        out_specs=pl.BlockSpec((None, rows, bn), lambda l, n: (l, 0, n)),
        compiler_params=_cparams("arbitrary", "arbitrary"),
        name="adaln",
    )(c_all, ada_w, ada_b.reshape(depth, 1, n6))


class _Rows:
    def __init__(self, batch, seq, m_sample, d, rt, layer=0):
        self.rt = rt
        self.d = d
        self.layer = layer
        self.tiles_per_batch = seq // rt
        self.n_prompt_tiles = batch * seq // rt
        self.n_tiles = self.n_prompt_tiles + m_sample // rt
        self.batch = batch
        self.halved = lambda: _Rows(batch, seq, m_sample, d, rt // 2, layer)
        self.at_layer = lambda l: _Rows(batch, seq, m_sample, d, rt, l)

    def mp_spec(self, which, width, col=False):
        tpb, nb = self.tiles_per_batch, self.batch
        if not col:
            return pl.BlockSpec((None, None, 1, width), lambda i: (jnp.minimum(i // tpb, nb - 1), which, 0, 0))
        return pl.BlockSpec((None, None, 1, width),
                            lambda n, i: (jnp.minimum(i // tpb, nb - 1), which, 0, n))

    def ms_spec(self, which, width, col=False):
        npt, layer = self.n_prompt_tiles, self.layer
        c0 = which * (self.d // width)
        if not col:
            return pl.BlockSpec((None, self.rt, width), lambda i: (layer, jnp.maximum(i - npt, 0), c0))
        return pl.BlockSpec((None, self.rt, width), lambda n, i: (layer, jnp.maximum(i - npt, 0), c0 + n))


def _pick_mod(i, rows, mp_ref, ms_ref):
    return jnp.where(i >= rows.n_prompt_tiles, ms_ref[...], mp_ref[...])


def _rms(x, g):
    return x * lax.rsqrt(jnp.mean(x * x, axis=-1, keepdims=True) + NORM_EPS) * g


def _norm_mod_kernel(rows, x_ref, g_ref, mp_sh, mp_sc, ms_sh, ms_sc, h_ref):
    i = pl.program_id(0)
    y = _rms(x_ref[...], g_ref[...])
    h = y * (1.0 + _pick_mod(i, rows, mp_sc, ms_sc)) + _pick_mod(i, rows, mp_sh, ms_sh)
    h_ref[...] = h.astype(h_ref.dtype)


def _norm_mod(rows, x, g, mod_p, mod_s, which_shift, which_scale):
    m, d = x.shape
    rt = rows.rt
    return pl.pallas_call(
        functools.partial(_norm_mod_kernel, rows),
        out_shape=jax.ShapeDtypeStruct((m, d), BF16),
        grid=(rows.n_tiles,),
        in_specs=[
            pl.BlockSpec((rt, d), lambda i: (i, 0)),
            pl.BlockSpec((1, d), lambda i: (0, 0)),
            rows.mp_spec(which_shift, d), rows.mp_spec(which_scale, d),
            rows.ms_spec(which_shift, d), rows.ms_spec(which_scale, d),
        ],
        out_specs=pl.BlockSpec((rt, d), lambda i: (i, 0)),
        compiler_params=_cparams("arbitrary"),
        name="norm_mod",
    )(x, g.reshape(1, d), mod_p, mod_p, mod_s, mod_s)


def _lane_pick(lane, k, col, acc):
    return jnp.where(lane == k, col, acc)


def _norm_route_kernel(rows, x_ref, g_ref, mp_sh, mp_sc, ms_sh, ms_sc, rw_ref, rb_ref,
                       h_ref, idx_ref, gw_ref, rank_ref, cnt_ref, carry_ref):
    i = pl.program_id(0)
    rt = rows.rt

    @pl.when(i == 0)
    def _():
        carry_ref[...] = jnp.zeros_like(carry_ref)

    y = _rms(x_ref[...], g_ref[...])
    h = y * (1.0 + _pick_mod(i, rows, mp_sc, ms_sc)) + _pick_mod(i, rows, mp_sh, ms_sh)
    h_ref[...] = _pack_bf16_pairs(h)

    logits = jnp.dot(h, rw_ref[...], preferred_element_type=F32, precision=HI) + rb_ref[...]
    lane = lax.broadcasted_iota(jnp.int32, (rt, LANES), 1)
    neg = jnp.float32(-jnp.inf)
    work = jnp.where(lane < N_EXPERTS, logits, neg)
    member = jnp.zeros((rt, LANES), F32)
    vals, idxs = [], []
    for _ in range(TOP_K):
        mx = jnp.max(work, axis=-1, keepdims=True)
        idx = jnp.min(jnp.where(work == mx, lane, LANES), axis=-1, keepdims=True)
        sel = lane == idx
        member = jnp.where(sel, 1.0, member)
        work = jnp.where(sel, neg, work)
        vals.append(mx)
        idxs.append(idx)
    exps = [jnp.exp(v - vals[0]) for v in vals]
    denom = exps[0]
    for e in exps[1:]:
        denom = denom + e

    r_i = lax.broadcasted_iota(jnp.int32, (rt, rt), 0)
    c_i = lax.broadcasted_iota(jnp.int32, (rt, rt), 1)
    below = jnp.where(r_i > c_i, 1.0, 0.0).astype(BF16)
    before = jnp.dot(below, member.astype(BF16), preferred_element_type=F32) + carry_ref[0:1, :]

    idx_out = jnp.zeros((rt, LANES), jnp.int32)
    gw_out = jnp.zeros((rt, LANES), F32)
    rank_out = jnp.zeros((rt, LANES), jnp.int32)
    for k in range(TOP_K):
        rank_k = jnp.sum(jnp.where(lane == idxs[k], before, 0.0), axis=-1, keepdims=True)
        idx_out = _lane_pick(lane, k, idxs[k], idx_out)
        gw_out = _lane_pick(lane, k, exps[k] / denom, gw_out)
        rank_out = _lane_pick(lane, k, rank_k.astype(jnp.int32), rank_out)
    idx_ref[...] = idx_out
    gw_ref[...] = gw_out
    rank_ref[...] = rank_out

    carry_ref[...] = carry_ref[...] + jnp.sum(member, axis=0, keepdims=True)
    cnt_ref[...] = carry_ref[...]


def _norm_route(rows, x, g, mod_p, mod_s, router_w, router_b):
    m, d = x.shape
    rt = rows.rt
    rw = jnp.pad(router_w, ((0, 0), (0, LANES - N_EXPERTS)))
    rb = jnp.pad(router_b, (0, LANES - N_EXPERTS)).reshape(1, LANES)
    row_spec = pl.BlockSpec((rt, LANES), lambda i: (i, 0))
    return pl.pallas_call(
        functools.partial(_norm_route_kernel, rows),
        out_shape=(
            jax.ShapeDtypeStruct((m, d // 2), F32),
            jax.ShapeDtypeStruct((m, LANES), jnp.int32),
            jax.ShapeDtypeStruct((m, LANES), F32),
            jax.ShapeDtypeStruct((m, LANES), jnp.int32),
            jax.ShapeDtypeStruct((SUBLANES, LANES), F32),
        ),
        grid=(rows.n_tiles,),
        in_specs=[
            pl.BlockSpec((rt, d), lambda i: (i, 0)),
            pl.BlockSpec((1, d), lambda i: (0, 0)),
            rows.mp_spec(3, d), rows.mp_spec(4, d),
            rows.ms_spec(3, d), rows.ms_spec(4, d),
            pl.BlockSpec((d, LANES), lambda i: (0, 0)),
            pl.BlockSpec((1, LANES), lambda i: (0, 0)),
        ],
        out_specs=(
            pl.BlockSpec((rt, d // 2), lambda i: (i, 0)),
            row_spec, row_spec, row_spec,
            pl.BlockSpec((SUBLANES, LANES), lambda i: (0, 0)),
        ),
        scratch_shapes=[pltpu.VMEM((SUBLANES, LANES), F32)],
        compiler_params=_cparams("arbitrary"),
        name="norm_route",
    )(x, g.reshape(1, d), mod_p, mod_p, mod_s, mod_s, rw, rb)


def _gelu(x):
    return 0.5 * x * (1.0 + lax.erf(x * (2.0 ** -0.5)))


def _mm_kernel(rows, epilogue, has_bias, *refs):
    refs = list(refs)
    x_ref, w_ref = refs[0], refs[1]
    pos = 2
    b_ref = None
    if has_bias:
        b_ref = refs[pos]
        pos += 1
    if epilogue == "resid":
        res_ref, mp_g, ms_g = refs[pos:pos + 3]
        pos += 3
    o_ref, wbf_ref = refs[pos], refs[pos + 1]
    i = pl.program_id(1)

    @pl.when(i == 0)
    def _():
        wbf_ref[...] = w_ref[...].astype(BF16)

    acc = jnp.dot(x_ref[...].astype(BF16), wbf_ref[...], preferred_element_type=F32)
    if has_bias:
        acc = acc + b_ref[...]
    if epilogue == "gelu":
        acc = _gelu(acc)
    elif epilogue == "resid":
        acc = res_ref[...] + _pick_mod(i, rows, mp_g, ms_g) * acc
    o_ref[...] = acc.astype(o_ref.dtype)


def _matmul(rows, x, w_all, layer, n, bias=None, epilogue="plain", resid=None, mod_p=None, mod_s=None,
            which_gate=None, bn=1024, out_dtype=F32, name="matmul"):
    m, k = x.shape
    bn = _tile(n, bn)
    rt = rows.rt
    in_specs = [pl.BlockSpec((rt, k), lambda j, i: (i, 0)),
                pl.BlockSpec((None, k, bn), lambda j, i: (layer, 0, j))]
    args = [x, w_all]
    if bias is not None:
        in_specs.append(pl.BlockSpec((1, bn), lambda j, i: (0, j)))
        args.append(bias.reshape(1, n))
    if epilogue == "resid":
        in_specs += [pl.BlockSpec((rt, bn), lambda j, i: (i, j)),
                     rows.mp_spec(which_gate, bn, col=True), rows.ms_spec(which_gate, bn, col=True)]
        args += [resid, mod_p, mod_s]
    return pl.pallas_call(
        functools.partial(_mm_kernel, rows, epilogue, bias is not None),
        out_shape=jax.ShapeDtypeStruct((m, n), out_dtype),
        grid=(n // bn, m // rt),
        in_specs=in_specs,
        out_specs=pl.BlockSpec((rt, bn), lambda j, i: (i, j)),
        scratch_shapes=[pltpu.VMEM((k, bn), BF16)],
        compiler_params=_cparams("arbitrary", "arbitrary"),
        name=name,
    )(*args)


def _log_sigmoid(x):
    return jnp.minimum(x, 0.0) - jnp.log1p(jnp.exp(-jnp.abs(x)))


def _gla_gate_logs(a, w2, b):
    return _log_sigmoid(_bdot(a, w2) + b) * (1.0 / GLA_GATE_NORMALIZER)


def _split3(x):
    hi = x.astype(BF16)
    r1 = x - hi.astype(F32)
    mid = r1.astype(BF16)
    lo = (r1 - mid.astype(F32)).astype(BF16)
    return hi, mid, lo


def _gla_cumsums(lg):
    c = lg.shape[0]
    r_i = lax.broadcasted_iota(jnp.int32, (c, c), 0)
    c_i = lax.broadcasted_iota(jnp.int32, (c, c), 1)
    tri = jnp.where(r_i >= c_i, 1.0, 0.0).astype(BF16)
    ones = jnp.ones((c, LANES), BF16)
    b = None
    tot = None
    for part in _split3(lg):
        pb = jnp.dot(tri, part, preferred_element_type=F32)
        pt = lax.dot_general(part, ones, (((0,), (0,)), ((), ())), preferred_element_type=F32)
        b = pb if b is None else b + pb
        tot = pt if tot is None else tot + pt
    return b, tot


def _gla_chunk(q, k, v, b, b_end_col, s, scale):
    c, dk = q.shape
    dv = v.shape[1]
    r_i = lax.broadcasted_iota(jnp.int32, (c, c), 0)
    c_i = lax.broadcasted_iota(jnp.int32, (c, c), 1)
    causal = r_i >= c_i
    b_end_row = b[c - 1:c, :]
    q_dec = (q * scale) * jnp.exp(b)
    k_inv = k * jnp.exp(-b)
    k_end = k * jnp.exp(b_end_row - b)
    att = lax.dot_general(q_dec.astype(BF16), k_inv.astype(BF16), (((1,), (1,)), ((), ())),
                          preferred_element_type=F32)
    att = jnp.where(causal, att, 0.0)
    o = _bdot(att, v) + _bdot(q_dec, s)
    decay = jnp.exp(b_end_col)
    decay = jnp.concatenate([decay] * (dv // LANES), axis=1)
    kv = lax.dot_general(k_end.astype(BF16), v.astype(BF16), (((0,), (0,)), ((), ())),
                         preferred_element_type=F32)
    return o, s * decay + kv


def _gla_out(o, g, onorm_g):
    return _rms(o, onorm_g) * _silu(g)


def _gla_prompt_kernel(dims, q_ref, k_ref, v_ref, g_ref, a_ref, w2_ref, bg_ref, on_ref,
                       og_ref, sout_ref, s_ref, lg_ref):
    heads, dk, dv, chunk, ts = dims
    step = pl.program_id(1)

    @pl.when(step == 0)
    def _():
        s_ref[...] = jnp.zeros_like(s_ref)

    lg_ref[...] = _gla_gate_logs(a_ref[...], w2_ref[...], bg_ref[...])
    scale = dk ** -0.5

    def body(ci, carry):
        r0 = pl.multiple_of(ci * chunk, chunk)
        rsl = pl.ds(r0, chunk)
        b_all, tot_all = _gla_cumsums(lg_ref[rsl, :])
        for h in range(heads):
            ksl = slice(h * dk, (h + 1) * dk)
            vsl = slice(h * dv, (h + 1) * dv)
            o, s_new = _gla_chunk(q_ref[rsl, ksl], k_ref[rsl, ksl], v_ref[rsl, vsl], b_all[:, ksl],
                                  tot_all[ksl, :], s_ref[h], scale)
            s_ref[h] = s_new
            og_ref[rsl, vsl] = _gla_out(o, g_ref[rsl, vsl], on_ref[...]).astype(og_ref.dtype)
        return carry

    lax.fori_loop(0, ts // chunk, body, 0)

    @pl.when(step == pl.num_programs(1) - 1)
    def _():
        sout_ref[...] = s_ref[...]


def _gla_prompt(z, a, w2p, b_gate, onorm_g, batch, seq, m_total, ts):
    d2 = z.shape[1]
    d = d2 // 3
    kd = d // 2
    heads = GLA_HEADS
    dk, dv = kd // heads, d // heads
    spb = seq // ts
    dims = (heads, dk, dv, GLA_CHUNK, ts)
    row = lambda n, s: n * spb + s
    return pl.pallas_call(
        functools.partial(_gla_prompt_kernel, dims),
        out_shape=(jax.ShapeDtypeStruct((m_total, d), F32),
                   jax.ShapeDtypeStruct((batch, heads, dk, dv), F32)),
        grid=(batch, spb),
        in_specs=[
            pl.BlockSpec((ts, kd), lambda n, s: (row(n, s), 0)),
            pl.BlockSpec((ts, kd), lambda n, s: (row(n, s), 1)),
            pl.BlockSpec((ts, d), lambda n, s: (row(n, s), 1)),
            pl.BlockSpec((ts, d), lambda n, s: (row(n, s), 2)),
            pl.BlockSpec((ts, LANES), lambda n, s: (row(n, s), 0)),
            pl.BlockSpec((LANES, kd), lambda n, s: (0, 0)),
            pl.BlockSpec((1, kd), lambda n, s: (0, 0)),
            pl.BlockSpec((1, dv), lambda n, s: (0, 0)),
        ],
        out_specs=(pl.BlockSpec((ts, d), lambda n, s: (row(n, s), 0)),
                   pl.BlockSpec((None, heads, dk, dv), lambda n, s: (n, 0, 0, 0))),
        scratch_shapes=[pltpu.VMEM((heads, dk, dv), F32), pltpu.VMEM((ts, kd), F32)],
        compiler_params=_cparams("arbitrary", "arbitrary"),
        name="gla_prompt",
    )(z, z, z, z, a, w2p, b_gate.reshape(1, kd), onorm_g.reshape(1, dv))


def _gla_sample_kernel(dims, n_alias, q_ref, k_ref, v_ref, g_ref, a_ref, w2_ref, bg_ref, on_ref, s0_ref, *refs):
    og_ref, sout_ref = refs[n_alias:]
    heads, dk, dv, nb, dec = dims
    c = nb * dec
    scale = dk ** -0.5
    lg_all = _gla_gate_logs(a_ref[...], w2_ref[...], bg_ref[...])
    row = lax.broadcasted_iota(jnp.int32, (c, 1), 0)
    mine = [(row >= b * dec) & (row < (b + 1) * dec) for b in range(nb)]
    sums = [_gla_cumsums(jnp.where(mine[b], lg_all, 0.0)) for b in range(nb)]
    for h in range(heads):
        ksl = slice(h * dk, (h + 1) * dk)
        vsl = slice(h * dv, (h + 1) * dv)
        q, k, v = q_ref[:, ksl], k_ref[:, ksl], v_ref[:, vsl]
        o_all = jnp.zeros((c, dv), F32)
        for b in range(nb):
            z = lambda t: jnp.where(mine[b], t, 0.0)
            o, s_new = _gla_chunk(z(q), z(k), z(v), sums[b][0][:, ksl], sums[b][1][ksl, :], s0_ref[b, h], scale)
            sout_ref[b, h] = s_new
            o_all = jnp.where(mine[b], o, o_all)
        og_ref[:, vsl] = _gla_out(o_all, g_ref[:, vsl], on_ref[...]).astype(og_ref.dtype)


def _gla_sample(z, a, w2p, b_gate, onorm_g, state_all, j, og, s_prev, m_prompt, dec_batch, dec_seq, nb):
    d2 = z.shape[1]
    d = d2 // 3
    kd = d // 2
    heads = GLA_HEADS
    dk, dv = kd // heads, d // heads
    c = nb * dec_seq
    r0 = m_prompt // c
    dims = (heads, dk, dv, nb, dec_seq)
    st_spec = pl.BlockSpec((None, nb, heads, dk, dv), lambda i: (j, i, 0, 0, 0))
    in_specs = [
        pl.BlockSpec((c, kd), lambda i: (r0 + i, 0)),
        pl.BlockSpec((c, kd), lambda i: (r0 + i, 1)),
        pl.BlockSpec((c, d), lambda i: (r0 + i, 1)),
        pl.BlockSpec((c, d), lambda i: (r0 + i, 2)),
        pl.BlockSpec((c, LANES), lambda i: (r0 + i, 0)),
        pl.BlockSpec((LANES, kd), lambda i: (0, 0)),
        pl.BlockSpec((1, kd), lambda i: (0, 0)),
        pl.BlockSpec((1, dv), lambda i: (0, 0)),
        st_spec,
        pl.BlockSpec(memory_space=pl.ANY),
    ]
    args = [z, z, z, z, a, w2p, b_gate.reshape(1, kd), onorm_g.reshape(1, dv), state_all, og]
    aliases = {len(args) - 1: 0}
    if s_prev is not None:
        in_specs.append(pl.BlockSpec(memory_space=pl.ANY))
        args.append(s_prev)
        aliases[len(args) - 1] = 1
    return pl.pallas_call(
        functools.partial(_gla_sample_kernel, dims, len(aliases)),
        out_shape=(jax.ShapeDtypeStruct(og.shape, og.dtype),
                   jax.ShapeDtypeStruct(state_all.shape, state_all.dtype)),
        grid=(dec_batch // nb,),
        in_specs=in_specs,
        out_specs=(pl.BlockSpec((c, d), lambda i: (r0 + i, 0)), st_spec),
        input_output_aliases=aliases,
        compiler_params=_cparams("arbitrary"),
        name="gla_sample",
    )(*args)


def _sgu_kernel(groups, u_ref, v_ref, lg_ref, lb_ref, w_ref, bias_ref, us_ref, vn_ref):
    v = v_ref[...]
    mu = jnp.mean(v, axis=-1, keepdims=True)
    vc = v - mu
    vn = vc * lax.rsqrt(jnp.mean(vc * vc, axis=-1, keepdims=True) + NORM_EPS) * lg_ref[...] + lb_ref[...]
    vn_ref[...] = vn
    gd = v.shape[1] // groups
    for g in range(groups):
        sl = slice(g * gd, (g + 1) * gd)
        s = _bdot(w_ref[g], vn[:, sl]) + bias_ref[:, sl]
        us_ref[:, sl] = (u_ref[:, sl] * s).astype(us_ref.dtype)


def _sgu_spatial(zs, ln_g, ln_b, wmix, biasfull, m_prompt, m_sample):
    m, d2 = zs.shape
    sd = d2 // 2
    ch = SGU_CHUNK
    npc = m_prompt // ch
    which = lambda c: jnp.where(c >= npc, 1, 0)
    return pl.pallas_call(
        functools.partial(_sgu_kernel, SGU_GROUPS),
        out_shape=(jax.ShapeDtypeStruct((m, sd), BF16), jax.ShapeDtypeStruct((m_sample, sd), F32)),
        grid=(m // ch,),
        in_specs=[
            pl.BlockSpec((ch, sd), lambda c: (c, 0)),
            pl.BlockSpec((ch, sd), lambda c: (c, 1)),
            pl.BlockSpec((1, sd), lambda c: (0, 0)),
            pl.BlockSpec((1, sd), lambda c: (0, 0)),
            pl.BlockSpec((None, SGU_GROUPS, ch, ch), lambda c: (which(c), 0, 0, 0)),
            pl.BlockSpec((None, ch, sd), lambda c: (which(c), 0, 0)),
        ],
        out_specs=(pl.BlockSpec((ch, sd), lambda c: (c, 0)),
                   pl.BlockSpec((ch, sd), lambda c: (jnp.maximum(c - npc, 0), 0))),
        compiler_params=_cparams("arbitrary"),
        name="sgu_spatial",
    )(zs, zs, ln_g.reshape(1, sd), ln_b.reshape(1, sd), wmix, biasfull)


def _sgu_mix_tables(w_s, b_s, dec_seq):
    groups, ch, _ = w_s.shape
    tril = jnp.tril(jnp.ones((ch, ch), bool))
    wp = jnp.where(tril, w_s, 0.0)
    small = jnp.where(jnp.tril(jnp.ones((dec_seq, dec_seq), bool)), w_s[:, :dec_seq, :dec_seq], 0.0)
    reps = ch // dec_seq
    eye = jnp.eye(reps, dtype=w_s.dtype)
    ws = jnp.einsum("ab,gts->gatbs", eye, small).reshape(groups, ch, ch)
    bp = b_s.T
    bs = jnp.tile(b_s[:, :dec_seq].T, (reps, 1))
    return jnp.stack([wp, ws]), jnp.stack([bp, bs])


class _Runs:
    def __init__(self, be, nu, first, run, nxt, last, nruns):
        self.be, self.nu, self.first, self.run, self.nxt, self.last, self.nruns = be, nu, first, run, nxt, last, nruns


N_RUN_SCALARS = 7


def _run_tables(block_e, n_used):
    nb = block_e.shape[0]
    pos = jnp.arange(nb, dtype=jnp.int32)
    live = pos < n_used[0]
    changed = jnp.concatenate([jnp.ones((1,), bool), block_e[1:] != block_e[:-1]]) & live
    run = jnp.cumsum(changed.astype(jnp.int32)) - 1
    nruns = jnp.sum(changed.astype(jnp.int32)).reshape(1)
    start = jnp.where(changed, pos, nb)
    after = jnp.concatenate([start[1:], jnp.full((1,), nb, jnp.int32)])
    nxt_pos = lax.cummin(after[::-1])[::-1]
    last = (nxt_pos >= nb).astype(jnp.int32)
    nxt = block_e[jnp.minimum(nxt_pos, nb - 1)]
    return (block_e, n_used, changed.astype(jnp.int32), run, nxt, last, nruns)


def _stream_expert_weights(runs, copies, on_arrival):
    j, b = pl.program_id(0), pl.program_id(1)
    nj = pl.num_programs(0)

    @pl.when(runs.first[b] == 1)
    def _():
        slot = lax.rem(j * runs.nruns[0] + runs.run[b], 2)

        @pl.when(jnp.logical_and(j == 0, b == 0))
        def _():
            for c in copies(runs.be[b], j, slot):
                c.start()

        for c in copies(runs.be[b], j, slot):
            c.wait()
        on_arrival(slot)

        @pl.when(runs.last[b] == 0)
        def _():
            for c in copies(runs.nxt[b], j, 1 - slot):
                c.start()

        @pl.when(jnp.logical_and(runs.last[b] == 1, j + 1 < nj))
        def _():
            for c in copies(runs.be[0], j + 1, 1 - slot):
                c.start()


def _moe_gu_kernel(layer, be, nu, first, run, nxt, last, nruns, x_ref, w_hbm, bg_ref, bu_ref, hid_ref,
                   wbuf, wg_bf, wu_bf, sem):
    runs = _Runs(be, nu, first, run, nxt, last, nruns)
    b = pl.program_id(1)
    live = b < nu[0]
    bnh = wg_bf.shape[1]
    de = w_hbm.shape[3] // 2

    def copies(e, jj, slot):
        c0 = pl.multiple_of(jj * bnh, LANES)
        return [pltpu.make_async_copy(w_hbm.at[layer, e, :, pl.ds(off + c0, bnh)], wbuf.at[slot, t], sem.at[slot, t])
                for t, off in enumerate((0, de))]

    def on_arrival(slot):
        wg_bf[...] = wbuf[slot, 0].astype(BF16)
        wu_bf[...] = wbuf[slot, 1].astype(BF16)

    _stream_expert_weights(runs, copies, on_arrival)

    @pl.when(live)
    def _():
        lo, hi = _unpack_bf16_pairs(x_ref[...])
        x = jnp.concatenate([lo, hi], axis=1).astype(BF16)
        gate = jnp.dot(x, wg_bf[...], preferred_element_type=F32) + bg_ref[...]
        up = jnp.dot(x, wu_bf[...], preferred_element_type=F32) + bu_ref[...]
        gate = jnp.minimum(gate, SWIGLU_LIMIT)
        up = jnp.clip(up, -SWIGLU_LIMIT, SWIGLU_LIMIT)
        glu = gate * (1.0 / (1.0 + jnp.exp(-gate * SWIGLU_ALPHA)))
        hid_ref[...] = ((up + 1.0) * glu).astype(hid_ref.dtype)

    @pl.when(jnp.logical_not(live))
    def _():
        hid_ref[...] = jnp.zeros_like(hid_ref)


def _moe_down_kernel(layer, be, nu, first, run, nxt, last, nruns, h_ref, w_hbm, b_ref, y_ref, wbuf, w_bf, sem):
    runs = _Runs(be, nu, first, run, nxt, last, nruns)
    b = pl.program_id(1)
    live = b < nu[0]
    bnd = w_bf.shape[1]

    def copies(e, jj, slot):
        c0 = pl.multiple_of(jj * bnd, LANES)
        return [pltpu.make_async_copy(w_hbm.at[layer, e, :, pl.ds(c0, bnd)], wbuf.at[slot], sem.at[slot])]

    def on_arrival(slot):
        w_bf[...] = wbuf[slot].astype(BF16)

    _stream_expert_weights(runs, copies, on_arrival)

    @pl.when(live)
    def _():
        y_ref[...] = jnp.dot(h_ref[...], w_bf[...], preferred_element_type=F32) + b_ref[...]

    @pl.when(jnp.logical_not(live))
    def _():
        y_ref[...] = jnp.zeros_like(y_ref)


def _moe_experts(x_sorted, block_e, n_used, layer, w_gu, b_gu, w_down, b_down, bnh, bnd):
    n_rows, dh = x_sorted.shape
    d = 2 * dh
    depth, n_exp, _, de2 = w_gu.shape
    de = de2 // 2
    bm = MOE_ROWS
    n_blocks = n_rows // bm
    nj = de // bnh
    b_gu4 = b_gu.reshape(depth, n_exp, 1, de2)
    tables = _run_tables(block_e, n_used)
    hid = pl.pallas_call(
        functools.partial(_moe_gu_kernel, layer),
        out_shape=jax.ShapeDtypeStruct((n_rows, de), BF16),
        grid_spec=pltpu.PrefetchScalarGridSpec(
            num_scalar_prefetch=N_RUN_SCALARS,
            grid=(nj, n_blocks),
            in_specs=[
                pl.BlockSpec((bm, dh), lambda j, b, *s: (b, 0)),
                pl.BlockSpec(memory_space=pl.ANY),
                pl.BlockSpec((None, None, 1, bnh), lambda j, b, be, *s: (layer, be[b], 0, j)),
                pl.BlockSpec((None, None, 1, bnh), lambda j, b, be, *s: (layer, be[b], 0, nj + j)),
            ],
            out_specs=pl.BlockSpec((bm, bnh), lambda j, b, *s: (b, j)),
            scratch_shapes=[pltpu.VMEM((2, 2, d, bnh), F32), pltpu.VMEM((d, bnh), BF16),
                            pltpu.VMEM((d, bnh), BF16), pltpu.SemaphoreType.DMA((2, 2))],
        ),
        compiler_params=_cparams("arbitrary", "arbitrary"),
        name="moe_gate_up",
    )(*tables, x_sorted, w_gu, b_gu4, b_gu4)

    njd = d // bnd
    return pl.pallas_call(
        functools.partial(_moe_down_kernel, layer),
        out_shape=jax.ShapeDtypeStruct((n_rows, d), F32),
        grid_spec=pltpu.PrefetchScalarGridSpec(
            num_scalar_prefetch=N_RUN_SCALARS,
            grid=(njd, n_blocks),
            in_specs=[
                pl.BlockSpec((bm, de), lambda j, b, *s: (b, 0)),
                pl.BlockSpec(memory_space=pl.ANY),
                pl.BlockSpec((None, None, 1, bnd), lambda j, b, be, *s: (layer, be[b], 0, j)),
            ],
            out_specs=pl.BlockSpec((bm, bnd), lambda j, b, *s: (b, j)),
            scratch_shapes=[pltpu.VMEM((2, de, bnd), F32), pltpu.VMEM((de, bnd), BF16),
                            pltpu.SemaphoreType.DMA((2,))],
        ),
        compiler_params=_cparams("arbitrary", "arbitrary"),
        name="moe_down",
    )(*tables, hid, w_down, b_down.reshape(depth, n_exp, 1, d))


def _combine_kernel(rows, final, *refs):
    x_ref, gw_ref, mp_g, ms_g = refs[0], refs[1], refs[2], refs[3]
    y_refs = refs[4:4 + TOP_K]
    pos = 4 + TOP_K
    i = pl.program_id(0)
    gw = gw_ref[...]
    f = y_refs[0][...] * gw[:, 0:1]
    for k in range(1, TOP_K):
        f = f + y_refs[k][...] * gw[:, k:k + 1]
    x_new = x_ref[...] + _pick_mod(i, rows, mp_g, ms_g) * f
    if not final:
        refs[pos][...] = x_new
        return
    fg_ref, op_ref, os_ref = refs[pos:pos + 3]
    y = _rms(x_new, fg_ref[...])

    @pl.when(i < rows.n_prompt_tiles)
    def _():
        op_ref[...] = y

    @pl.when(i >= rows.n_prompt_tiles)
    def _():
        os_ref[...] = y


def _combine(rows, x, gw, mod_p, mod_s, yk, final_g=None):
    m, d = x.shape
    rt = rows.rt
    nt = rows.n_tiles
    final = final_g is not None
    in_specs = [
        pl.BlockSpec((rt, d), lambda i: (i, 0)),
        pl.BlockSpec((rt, LANES), lambda i: (i, 0)),
        rows.mp_spec(5, d), rows.ms_spec(5, d),
    ] + [pl.BlockSpec((rt, d), functools.partial(lambda k, i: (k * nt + i, 0), k)) for k in range(TOP_K)]
    args = [x, gw, mod_p, mod_s] + [yk] * TOP_K
    out_shape = jax.ShapeDtypeStruct((m, d), F32)
    out_specs = pl.BlockSpec((rt, d), lambda i: (i, 0))
    if final:
        in_specs.append(pl.BlockSpec((1, d), lambda i: (0, 0)))
        args.append(final_g.reshape(1, d))
        npt = rows.n_prompt_tiles
        out_shape = (jax.ShapeDtypeStruct((npt * rt, d), F32), jax.ShapeDtypeStruct((m - npt * rt, d), F32))
        out_specs = (pl.BlockSpec((rt, d), lambda i: (jnp.minimum(i, npt - 1), 0)),
                     pl.BlockSpec((rt, d), lambda i: (jnp.maximum(i - npt, 0), 0)))
    return pl.pallas_call(
        functools.partial(_combine_kernel, rows, final),
        out_shape=out_shape,
        grid=(nt,),
        in_specs=in_specs,
        out_specs=out_specs,
        compiler_params=_cparams("arbitrary"),
        name="moe_combine",
    )(*args)


def _moe_layer(rows, x, g2, mod_p, mod_s, layer, router_w, router_b, w_gu, b_gu, w_down, b_down, final_g):
    m, d = x.shape
    h, idx, gw, rank, cnt = _norm_route(rows, x, g2, mod_p, mod_s, router_w[layer], router_b[layer])
    bm = MOE_ROWS
    tk = m * TOP_K
    n_blocks = -(-(tk + N_EXPERTS * (bm - 1)) // bm)
    n_rows = n_blocks * bm
    counts = cnt[0, :N_EXPERTS].astype(jnp.int32)
    padded = (counts + bm - 1) // bm * bm
    pad_end = jnp.cumsum(padded)
    pad_start = pad_end - padded
    dest = (pad_start[idx[:, :TOP_K]] + rank[:, :TOP_K]).T.reshape(tk)
    tok = jnp.tile(jnp.arange(m, dtype=jnp.int32), TOP_K)
    row_tok = jnp.zeros((n_rows,), jnp.int32).at[dest].set(tok, unique_indices=True, indices_are_sorted=False)
    block_start = jnp.arange(n_blocks, dtype=jnp.int32) * bm
    block_e = jnp.minimum(jnp.sum(pad_end[None, :] <= block_start[:, None], axis=1), N_EXPERTS - 1).astype(jnp.int32)
    n_used = (pad_end[-1] // bm).astype(jnp.int32).reshape(1)
    x_sorted = jnp.take(h, row_tok, axis=0, mode="clip")
    de = w_gu.shape[3] // 2
    y = _moe_experts(x_sorted, block_e, n_used, layer, w_gu, b_gu, w_down, b_down, min(1024, de), min(2048, d))
    yk = jnp.take(y, dest, axis=0, mode="clip")
    return _combine(rows.halved(), x, gw, mod_p, mod_s, yk, final_g)


def kernel(x_prompt, x_sample, state_gla, c_prompt, c_sample, ada_w, ada_b, norm1_g, norm2_g, final_g,
           gla_w_in, gla_w_gate2, gla_b_gate, gla_onorm_g, gla_w_out,
           sgu_w_in, sgu_b_in, sgu_ln_g, sgu_ln_b, sgu_w_s, sgu_b_s, sgu_w_out, sgu_b_out,
           router_w, router_b, exp_w_gu, exp_b_gu, exp_w_down, exp_b_down):
    batch, seq, d = x_prompt.shape
    dec_batch, dec_seq, _ = x_sample.shape
    depth = ada_w.shape[0]
    m_p, m_s = batch * seq, dec_batch * dec_seq
    m = m_p + m_s
    rt = min(ROW_TILE, m_s)
    rows0 = _Rows(batch, seq, m_s, d, rt)
    kd = d // 2
    n_qkvg = 2 * kd + 2 * d

    n_c = m_s + batch
    n_c_pad = -(-n_c // SUBLANES) * SUBLANES
    c_all = jnp.pad(jnp.concatenate([jnp.repeat(c_sample, dec_seq, axis=0), c_prompt], axis=0),
                    ((0, n_c_pad - n_c), (0, 0)))
    mod_all = _adaln_all(c_all, ada_w, ada_b, min(1024, d))
    mod_s = mod_all

    x = jnp.concatenate([x_prompt.reshape(m_p, d), x_sample.reshape(m_s, d)], axis=0)
    s_out_sample = None
    s_out_prompt, v_out_sample = [], []
    gla_nb = SUBLANES // dec_seq
    gla_ts = min(256, seq)

    for i in range(depth):
        rows = rows0.at_layer(i)
        mod_p = mod_all[i, m_s:n_c].reshape(batch, 6, 1, d)
        h = _norm_mod(rows, x, norm1_g[i], mod_p, mod_s, 0, 1)
        j = i // 2
        if i % 2 == 0:
            z = _matmul(rows, h, gla_w_in, j, n_qkvg, name="gla_in")
            w_a = jnp.pad(gla_w_in[j, :, n_qkvg:], ((0, 0), (0, LANES - GLA_GATE_RANK)))[None]
            a = _matmul(rows, h, w_a, 0, LANES, name="gla_gate_in")
            w2p = jnp.pad(gla_w_gate2[j], ((0, LANES - GLA_GATE_RANK), (0, 0)))
            og, s_p = _gla_prompt(z, a, w2p, gla_b_gate[j], gla_onorm_g[j], batch, seq, m, gla_ts)
            og, s_out_sample = _gla_sample(z, a, w2p, gla_b_gate[j], gla_onorm_g[j], state_gla, j, og,
                                           s_out_sample, m_p, dec_batch, dec_seq, gla_nb)
            s_out_prompt.append(s_p)
            x = _matmul(rows, og, gla_w_out, j, d, epilogue="resid", resid=x, mod_p=mod_p, mod_s=mod_s,
                        which_gate=2, name="gla_out")
        else:
            zs = _matmul(rows, h, sgu_w_in, j, 2 * d, bias=sgu_b_in[j], epilogue="gelu", name="sgu_in")
            wmix, bmix = _sgu_mix_tables(sgu_w_s[j], sgu_b_s[j], dec_seq)
            biasfull = jnp.repeat(bmix, d // SGU_GROUPS, axis=2)
            us, vn_s = _sgu_spatial(zs, sgu_ln_g[j], sgu_ln_b[j], wmix, biasfull, m_p, m_s)
            v_out_sample.append(vn_s.reshape(dec_batch, dec_seq, d))
            x = _matmul(rows, us, sgu_w_out, j, d, bias=sgu_b_out[j], epilogue="resid", resid=x, mod_p=mod_p,
                        mod_s=mod_s, which_gate=2, name="sgu_out")
        x = _moe_layer(rows, x, norm2_g[i], mod_p, mod_s, i, router_w, router_b, exp_w_gu, exp_b_gu,
                       exp_w_down, exp_b_down, final_g if i == depth - 1 else None)

    y_prompt = x[0].reshape(batch, seq, d)
    y_sample = x[1].reshape(dec_batch, dec_seq, d)
    return (y_prompt, y_sample, jnp.stack(s_out_prompt), s_out_sample, jnp.stack(v_out_sample))
```

```python
import functools

import jax
import jax.numpy as jnp
from jax import lax
from jax.experimental import pallas as pl
from jax.experimental.pallas import tpu as pltpu

F32 = jnp.float32
BF16 = jnp.bfloat16
U32 = jnp.uint32

GLA_HEADS = 4
GLA_GATE_RANK = 16
GLA_GATE_NORMALIZER = 16.0
GLA_CHUNK = 32
SGU_GROUPS = 8
SGU_CHUNK = 128
N_EXPERTS = 32
TOP_K = 4
SWIGLU_LIMIT = 7.0
SWIGLU_ALPHA = 1.702
NORM_EPS = 1e-6

LANES = 128
SUBLANES = 8
VMEM_LIMIT = 56 * 1024 * 1024
ROW_TILE = 512
MOE_ROWS = 256
HI = lax.Precision.HIGHEST


def _cparams(*sem):
    return pltpu.CompilerParams(dimension_semantics=sem, vmem_limit_bytes=VMEM_LIMIT)


def _tile(n, target):
    t = min(n, target) // LANES * LANES
    while n % t:
        t -= LANES
    return t


def _bdot(a, b):
    return jnp.dot(a.astype(BF16), b.astype(BF16), preferred_element_type=F32)


def _silu(x):
    return x * (1.0 / (1.0 + jnp.exp(-x)))


def _pack_bf16_pairs(x):
    half = x.shape[1] // 2
    bits = pltpu.bitcast(x.astype(BF16).astype(F32), U32)
    return pltpu.bitcast((bits[:, :half] >> 16) | bits[:, half:], F32)


def _unpack_bf16_pairs(p):
    p = pltpu.bitcast(p, U32)
    return pltpu.bitcast(p << 16, F32), pltpu.bitcast(p & jnp.uint32(0xFFFF0000), F32)


def _adaln_kernel(c_ref, w_ref, b_ref, o_ref):
    o_ref[...] = _bdot(_silu(c_ref[...]), w_ref[...]) + b_ref[...]


def _adaln_all(c_all, ada_w, ada_b, bn):
    depth, d, n6 = ada_w.shape
    rows = c_all.shape[0]
    return pl.pallas_call(
        _adaln_kernel,
        out_shape=jax.ShapeDtypeStruct((depth, rows, n6), F32),
        grid=(depth, n6 // bn),
        in_specs=[
            pl.BlockSpec((rows, d), lambda l, n: (0, 0)),
            pl.BlockSpec((None, d, bn), lambda l, n: (l, 0, n)),
            pl.BlockSpec((None, 1, bn), lambda l, n: (l, 0, n)),
        ],
        out_specs=pl.BlockSpec((None, rows, bn), lambda l, n: (l, 0, n)),
        compiler_params=_cparams("arbitrary", "arbitrary"),
        name="adaln",
    )(c_all, ada_w, ada_b.reshape(depth, 1, n6))


class _Rows:
    def __init__(self, batch, seq, m_sample, d, rt, layer=0):
        self.rt = rt
        self.d = d
        self.layer = layer
        self.tiles_per_batch = seq // rt
        self.n_prompt_tiles = batch * seq // rt
        self.n_tiles = self.n_prompt_tiles + m_sample // rt
        self.batch = batch
        self.halved = lambda: _Rows(batch, seq, m_sample, d, rt // 2, layer)
        self.at_layer = lambda l: _Rows(batch, seq, m_sample, d, rt, l)

    def mp_spec(self, which, width, col=False):
        tpb, nb = self.tiles_per_batch, self.batch
        if not col:
            return pl.BlockSpec((None, None, 1, width), lambda i: (jnp.minimum(i // tpb, nb - 1), which, 0, 0))
        return pl.BlockSpec((None, None, 1, width),
                            lambda n, i: (jnp.minimum(i // tpb, nb - 1), which, 0, n))

    def ms_spec(self, which, width, col=False):
        npt, layer = self.n_prompt_tiles, self.layer
        c0 = which * (self.d // width)
        if not col:
            return pl.BlockSpec((None, self.rt, width), lambda i: (layer, jnp.maximum(i - npt, 0), c0))
        return pl.BlockSpec((None, self.rt, width), lambda n, i: (layer, jnp.maximum(i - npt, 0), c0 + n))


def _pick_mod(i, rows, mp_ref, ms_ref):
    return jnp.where(i >= rows.n_prompt_tiles, ms_ref[...], mp_ref[...])


def _rms(x, g):
    return x * lax.rsqrt(jnp.mean(x * x, axis=-1, keepdims=True) + NORM_EPS) * g


def _norm_mod_kernel(rows, x_ref, g_ref, mp_sh, mp_sc, ms_sh, ms_sc, h_ref):
    i = pl.program_id(0)
    y = _rms(x_ref[...], g_ref[...])
    h = y * (1.0 + _pick_mod(i, rows, mp_sc, ms_sc)) + _pick_mod(i, rows, mp_sh, ms_sh)
    h_ref[...] = h.astype(h_ref.dtype)


def _norm_mod(rows, x, g, mod_p, mod_s, which_shift, which_scale):
    m, d = x.shape
    rt = rows.rt
    return pl.pallas_call(
        functools.partial(_norm_mod_kernel, rows),
        out_shape=jax.ShapeDtypeStruct((m, d), BF16),
        grid=(rows.n_tiles,),
        in_specs=[
            pl.BlockSpec((rt, d), lambda i: (i, 0)),
            pl.BlockSpec((1, d), lambda i: (0, 0)),
            rows.mp_spec(which_shift, d), rows.mp_spec(which_scale, d),
            rows.ms_spec(which_shift, d), rows.ms_spec(which_scale, d),
        ],
        out_specs=pl.BlockSpec((rt, d), lambda i: (i, 0)),
        compiler_params=_cparams("arbitrary"),
        name="norm_mod",
    )(x, g.reshape(1, d), mod_p, mod_p, mod_s, mod_s)


def _lane_pick(lane, k, col, acc):
    return jnp.where(lane == k, col, acc)


def _norm_route_kernel(rows, x_ref, g_ref, mp_sh, mp_sc, ms_sh, ms_sc, rw_ref, rb_ref,
                       h_ref, idx_ref, gw_ref, rank_ref, cnt_ref, carry_ref):
    i = pl.program_id(0)
    rt = rows.rt

    @pl.when(i == 0)
    def _():
        carry_ref[...] = jnp.zeros_like(carry_ref)

    y = _rms(x_ref[...], g_ref[...])
    h = y * (1.0 + _pick_mod(i, rows, mp_sc, ms_sc)) + _pick_mod(i, rows, mp_sh, ms_sh)
    h_ref[...] = _pack_bf16_pairs(h)

    logits = jnp.dot(h, rw_ref[...], preferred_element_type=F32, precision=HI) + rb_ref[...]
    lane = lax.broadcasted_iota(jnp.int32, (rt, LANES), 1)
    neg = jnp.float32(-jnp.inf)
    work = jnp.where(lane < N_EXPERTS, logits, neg)
    member = jnp.zeros((rt, LANES), F32)
    vals, idxs = [], []
    for _ in range(TOP_K):
        mx = jnp.max(work, axis=-1, keepdims=True)
        idx = jnp.min(jnp.where(work == mx, lane, LANES), axis=-1, keepdims=True)
        sel = lane == idx
        member = jnp.where(sel, 1.0, member)
        work = jnp.where(sel, neg, work)
        vals.append(mx)
        idxs.append(idx)
    exps = [jnp.exp(v - vals[0]) for v in vals]
    denom = exps[0]
    for e in exps[1:]:
        denom = denom + e

    r_i = lax.broadcasted_iota(jnp.int32, (rt, rt), 0)
    c_i = lax.broadcasted_iota(jnp.int32, (rt, rt), 1)
    below = jnp.where(r_i > c_i, 1.0, 0.0).astype(BF16)
    before = jnp.dot(below, member.astype(BF16), preferred_element_type=F32) + carry_ref[0:1, :]

    idx_out = jnp.zeros((rt, LANES), jnp.int32)
    gw_out = jnp.zeros((rt, LANES), F32)
    rank_out = jnp.zeros((rt, LANES), jnp.int32)
    for k in range(TOP_K):
        rank_k = jnp.sum(jnp.where(lane == idxs[k], before, 0.0), axis=-1, keepdims=True)
        idx_out = _lane_pick(lane, k, idxs[k], idx_out)
        gw_out = _lane_pick(lane, k, exps[k] / denom, gw_out)
        rank_out = _lane_pick(lane, k, rank_k.astype(jnp.int32), rank_out)
    idx_ref[...] = idx_out
    gw_ref[...] = gw_out
    rank_ref[...] = rank_out

    carry_ref[...] = carry_ref[...] + jnp.sum(member, axis=0, keepdims=True)
    cnt_ref[...] = carry_ref[...]


def _norm_route(rows, x, g, mod_p, mod_s, router_w, router_b):
    m, d = x.shape
    rt = rows.rt
    rw = jnp.pad(router_w, ((0, 0), (0, LANES - N_EXPERTS)))
    rb = jnp.pad(router_b, (0, LANES - N_EXPERTS)).reshape(1, LANES)
    row_spec = pl.BlockSpec((rt, LANES), lambda i: (i, 0))
    return pl.pallas_call(
        functools.partial(_norm_route_kernel, rows),
        out_shape=(
            jax.ShapeDtypeStruct((m, d // 2), F32),
            jax.ShapeDtypeStruct((m, LANES), jnp.int32),
            jax.ShapeDtypeStruct((m, LANES), F32),
            jax.ShapeDtypeStruct((m, LANES), jnp.int32),
            jax.ShapeDtypeStruct((SUBLANES, LANES), F32),
        ),
        grid=(rows.n_tiles,),
        in_specs=[
            pl.BlockSpec((rt, d), lambda i: (i, 0)),
            pl.BlockSpec((1, d), lambda i: (0, 0)),
            rows.mp_spec(3, d), rows.mp_spec(4, d),
            rows.ms_spec(3, d), rows.ms_spec(4, d),
            pl.BlockSpec((d, LANES), lambda i: (0, 0)),
            pl.BlockSpec((1, LANES), lambda i: (0, 0)),
        ],
        out_specs=(
            pl.BlockSpec((rt, d // 2), lambda i: (i, 0)),
            row_spec, row_spec, row_spec,
            pl.BlockSpec((SUBLANES, LANES), lambda i: (0, 0)),
        ),
        scratch_shapes=[pltpu.VMEM((SUBLANES, LANES), F32)],
        compiler_params=_cparams("arbitrary"),
        name="norm_route",
    )(x, g.reshape(1, d), mod_p, mod_p, mod_s, mod_s, rw, rb)


def _gelu(x):
    return 0.5 * x * (1.0 + lax.erf(x * (2.0 ** -0.5)))


def _mm_kernel(rows, epilogue, has_bias, *refs):
    refs = list(refs)
    x_ref, w_ref = refs[0], refs[1]
    pos = 2
    b_ref = None
    if has_bias:
        b_ref = refs[pos]
        pos += 1
    if epilogue == "resid":
        res_ref, mp_g, ms_g = refs[pos:pos + 3]
        pos += 3
    o_ref, wbf_ref = refs[pos], refs[pos + 1]
    i = pl.program_id(1)

    @pl.when(i == 0)
    def _():
        wbf_ref[...] = w_ref[...].astype(BF16)

    acc = jnp.dot(x_ref[...].astype(BF16), wbf_ref[...], preferred_element_type=F32)
    if has_bias:
        acc = acc + b_ref[...]
    if epilogue == "gelu":
        acc = _gelu(acc)
    elif epilogue == "resid":
        acc = res_ref[...] + _pick_mod(i, rows, mp_g, ms_g) * acc
    o_ref[...] = acc.astype(o_ref.dtype)


def _matmul(rows, x, w_all, layer, n, bias=None, epilogue="plain", resid=None, mod_p=None, mod_s=None,
            which_gate=None, bn=1024, out_dtype=F32, name="matmul"):
    m, k = x.shape
    bn = _tile(n, bn)
    rt = rows.rt
    in_specs = [pl.BlockSpec((rt, k), lambda j, i: (i, 0)),
                pl.BlockSpec((None, k, bn), lambda j, i: (layer, 0, j))]
    args = [x, w_all]
    if bias is not None:
        in_specs.append(pl.BlockSpec((1, bn), lambda j, i: (0, j)))
        args.append(bias.reshape(1, n))
    if epilogue == "resid":
        in_specs += [pl.BlockSpec((rt, bn), lambda j, i: (i, j)),
                     rows.mp_spec(which_gate, bn, col=True), rows.ms_spec(which_gate, bn, col=True)]
        args += [resid, mod_p, mod_s]
    return pl.pallas_call(
        functools.partial(_mm_kernel, rows, epilogue, bias is not None),
        out_shape=jax.ShapeDtypeStruct((m, n), out_dtype),
        grid=(n // bn, m // rt),
        in_specs=in_specs,
        out_specs=pl.BlockSpec((rt, bn), lambda j, i: (i, j)),
        scratch_shapes=[pltpu.VMEM((k, bn), BF16)],
        compiler_params=_cparams("arbitrary", "arbitrary"),
        name=name,
    )(*args)


def _log_sigmoid(x):
    return jnp.minimum(x, 0.0) - jnp.log1p(jnp.exp(-jnp.abs(x)))


def _gla_gate_logs(a, w2, b):
    return _log_sigmoid(_bdot(a, w2) + b) * (1.0 / GLA_GATE_NORMALIZER)


def _split3(x):
    hi = x.astype(BF16)
    r1 = x - hi.astype(F32)
    mid = r1.astype(BF16)
    lo = (r1 - mid.astype(F32)).astype(BF16)
    return hi, mid, lo


def _gla_cumsums(lg):
    c = lg.shape[0]
    r_i = lax.broadcasted_iota(jnp.int32, (c, c), 0)
    c_i = lax.broadcasted_iota(jnp.int32, (c, c), 1)
    tri = jnp.where(r_i >= c_i, 1.0, 0.0).astype(BF16)
    ones = jnp.ones((c, LANES), BF16)
    b = None
    tot = None
    for part in _split3(lg):
        pb = jnp.dot(tri, part, preferred_element_type=F32)
        pt = lax.dot_general(part, ones, (((0,), (0,)), ((), ())), preferred_element_type=F32)
        b = pb if b is None else b + pb
        tot = pt if tot is None else tot + pt
    return b, tot


def _gla_chunk(q, k, v, b, b_end_col, s, scale):
    c, dk = q.shape
    dv = v.shape[1]
    r_i = lax.broadcasted_iota(jnp.int32, (c, c), 0)
    c_i = lax.broadcasted_iota(jnp.int32, (c, c), 1)
    causal = r_i >= c_i
    b_end_row = b[c - 1:c, :]
    q_dec = (q * scale) * jnp.exp(b)
    k_inv = k * jnp.exp(-b)
    k_end = k * jnp.exp(b_end_row - b)
    att = lax.dot_general(q_dec.astype(BF16), k_inv.astype(BF16), (((1,), (1,)), ((), ())),
                          preferred_element_type=F32)
    att = jnp.where(causal, att, 0.0)
    o = _bdot(att, v) + _bdot(q_dec, s)
    decay = jnp.exp(b_end_col)
    decay = jnp.concatenate([decay] * (dv // LANES), axis=1)
    kv = lax.dot_general(k_end.astype(BF16), v.astype(BF16), (((0,), (0,)), ((), ())),
                         preferred_element_type=F32)
    return o, s * decay + kv


def _gla_out(o, g, onorm_g):
    return _rms(o, onorm_g) * _silu(g)


def _gla_prompt_kernel(dims, q_ref, k_ref, v_ref, g_ref, a_ref, w2_ref, bg_ref, on_ref,
                       og_ref, sout_ref, s_ref, lg_ref):
    heads, dk, dv, chunk, ts = dims
    step = pl.program_id(1)

    @pl.when(step == 0)
    def _():
        s_ref[...] = jnp.zeros_like(s_ref)

    lg_ref[...] = _gla_gate_logs(a_ref[...], w2_ref[...], bg_ref[...])
    scale = dk ** -0.5

    def body(ci, carry):
        r0 = pl.multiple_of(ci * chunk, chunk)
        rsl = pl.ds(r0, chunk)
        b_all, tot_all = _gla_cumsums(lg_ref[rsl, :])
        for h in range(heads):
            ksl = slice(h * dk, (h + 1) * dk)
            vsl = slice(h * dv, (h + 1) * dv)
            o, s_new = _gla_chunk(q_ref[rsl, ksl], k_ref[rsl, ksl], v_ref[rsl, vsl], b_all[:, ksl],
                                  tot_all[ksl, :], s_ref[h], scale)
            s_ref[h] = s_new
            og_ref[rsl, vsl] = _gla_out(o, g_ref[rsl, vsl], on_ref[...]).astype(og_ref.dtype)
        return carry

    lax.fori_loop(0, ts // chunk, body, 0)

    @pl.when(step == pl.num_programs(1) - 1)
    def _():
        sout_ref[...] = s_ref[...]


def _gla_prompt(z, a, w2p, b_gate, onorm_g, batch, seq, m_total, ts):
    d2 = z.shape[1]
    d = d2 // 3
    kd = d // 2
    heads = GLA_HEADS
    dk, dv = kd // heads, d // heads
    spb = seq // ts
    dims = (heads, dk, dv, GLA_CHUNK, ts)
    row = lambda n, s: n * spb + s
    return pl.pallas_call(
        functools.partial(_gla_prompt_kernel, dims),
        out_shape=(jax.ShapeDtypeStruct((m_total, d), F32),
                   jax.ShapeDtypeStruct((batch, heads, dk, dv), F32)),
        grid=(batch, spb),
        in_specs=[
            pl.BlockSpec((ts, kd), lambda n, s: (row(n, s), 0)),
            pl.BlockSpec((ts, kd), lambda n, s: (row(n, s), 1)),
            pl.BlockSpec((ts, d), lambda n, s: (row(n, s), 1)),
            pl.BlockSpec((ts, d), lambda n, s: (row(n, s), 2)),
            pl.BlockSpec((ts, LANES), lambda n, s: (row(n, s), 0)),
            pl.BlockSpec((LANES, kd), lambda n, s: (0, 0)),
            pl.BlockSpec((1, kd), lambda n, s: (0, 0)),
            pl.BlockSpec((1, dv), lambda n, s: (0, 0)),
        ],
        out_specs=(pl.BlockSpec((ts, d), lambda n, s: (row(n, s), 0)),
                   pl.BlockSpec((None, heads, dk, dv), lambda n, s: (n, 0, 0, 0))),
        scratch_shapes=[pltpu.VMEM((heads, dk, dv), F32), pltpu.VMEM((ts, kd), F32)],
        compiler_params=_cparams("arbitrary", "arbitrary"),
        name="gla_prompt",
    )(z, z, z, z, a, w2p, b_gate.reshape(1, kd), onorm_g.reshape(1, dv))


def _gla_sample_kernel(dims, n_alias, q_ref, k_ref, v_ref, g_ref, a_ref, w2_ref, bg_ref, on_ref, s0_ref, *refs):
    og_ref, sout_ref = refs[n_alias:]
    heads, dk, dv, nb, dec = dims
    c = nb * dec
    scale = dk ** -0.5
    lg_all = _gla_gate_logs(a_ref[...], w2_ref[...], bg_ref[...])
    row = lax.broadcasted_iota(jnp.int32, (c, 1), 0)
    mine = [(row >= b * dec) & (row < (b + 1) * dec) for b in range(nb)]
    sums = [_gla_cumsums(jnp.where(mine[b], lg_all, 0.0)) for b in range(nb)]
    for h in range(heads):
        ksl = slice(h * dk, (h + 1) * dk)
        vsl = slice(h * dv, (h + 1) * dv)
        q, k, v = q_ref[:, ksl], k_ref[:, ksl], v_ref[:, vsl]
        o_all = jnp.zeros((c, dv), F32)
        for b in range(nb):
            z = lambda t: jnp.where(mine[b], t, 0.0)
            o, s_new = _gla_chunk(z(q), z(k), z(v), sums[b][0][:, ksl], sums[b][1][ksl, :], s0_ref[b, h], scale)
            sout_ref[b, h] = s_new
            o_all = jnp.where(mine[b], o, o_all)
        og_ref[:, vsl] = _gla_out(o_all, g_ref[:, vsl], on_ref[...]).astype(og_ref.dtype)


def _gla_sample(z, a, w2p, b_gate, onorm_g, state_all, j, og, s_prev, m_prompt, dec_batch, dec_seq, nb):
    d2 = z.shape[1]
    d = d2 // 3
    kd = d // 2
    heads = GLA_HEADS
    dk, dv = kd // heads, d // heads
    c = nb * dec_seq
    r0 = m_prompt // c
    dims = (heads, dk, dv, nb, dec_seq)
    st_spec = pl.BlockSpec((None, nb, heads, dk, dv), lambda i: (j, i, 0, 0, 0))
    in_specs = [
        pl.BlockSpec((c, kd), lambda i: (r0 + i, 0)),
        pl.BlockSpec((c, kd), lambda i: (r0 + i, 1)),
        pl.BlockSpec((c, d), lambda i: (r0 + i, 1)),
        pl.BlockSpec((c, d), lambda i: (r0 + i, 2)),
        pl.BlockSpec((c, LANES), lambda i: (r0 + i, 0)),
        pl.BlockSpec((LANES, kd), lambda i: (0, 0)),
        pl.BlockSpec((1, kd), lambda i: (0, 0)),
        pl.BlockSpec((1, dv), lambda i: (0, 0)),
        st_spec,
        pl.BlockSpec(memory_space=pl.ANY),
    ]
    args = [z, z, z, z, a, w2p, b_gate.reshape(1, kd), onorm_g.reshape(1, dv), state_all, og]
    aliases = {len(args) - 1: 0}
    if s_prev is not None:
        in_specs.append(pl.BlockSpec(memory_space=pl.ANY))
        args.append(s_prev)
        aliases[len(args) - 1] = 1
    return pl.pallas_call(
        functools.partial(_gla_sample_kernel, dims, len(aliases)),
        out_shape=(jax.ShapeDtypeStruct(og.shape, og.dtype),
                   jax.ShapeDtypeStruct(state_all.shape, state_all.dtype)),
        grid=(dec_batch // nb,),
        in_specs=in_specs,
        out_specs=(pl.BlockSpec((c, d), lambda i: (r0 + i, 0)), st_spec),
        input_output_aliases=aliases,
        compiler_params=_cparams("arbitrary"),
        name="gla_sample",
    )(*args)


def _sgu_kernel(groups, u_ref, v_ref, lg_ref, lb_ref, w_ref, bias_ref, us_ref, vn_ref):
    v = v_ref[...]
    mu = jnp.mean(v, axis=-1, keepdims=True)
    vc = v - mu
    vn = vc * lax.rsqrt(jnp.mean(vc * vc, axis=-1, keepdims=True) + NORM_EPS) * lg_ref[...] + lb_ref[...]
    vn_ref[...] = vn
    gd = v.shape[1] // groups
    for g in range(groups):
        sl = slice(g * gd, (g + 1) * gd)
        s = _bdot(w_ref[g], vn[:, sl]) + bias_ref[:, sl]
        us_ref[:, sl] = (u_ref[:, sl] * s).astype(us_ref.dtype)


def _sgu_spatial(zs, ln_g, ln_b, wmix, biasfull, m_prompt, m_sample):
    m, d2 = zs.shape
    sd = d2 // 2
    ch = SGU_CHUNK
    npc = m_prompt // ch
    which = lambda c: jnp.where(c >= npc, 1, 0)
    return pl.pallas_call(
        functools.partial(_sgu_kernel, SGU_GROUPS),
        out_shape=(jax.ShapeDtypeStruct((m, sd), BF16), jax.ShapeDtypeStruct((m_sample, sd), F32)),
        grid=(m // ch,),
        in_specs=[
            pl.BlockSpec((ch, sd), lambda c: (c, 0)),
            pl.BlockSpec((ch, sd), lambda c: (c, 1)),
            pl.BlockSpec((1, sd), lambda c: (0, 0)),
            pl.BlockSpec((1, sd), lambda c: (0, 0)),
            pl.BlockSpec((None, SGU_GROUPS, ch, ch), lambda c: (which(c), 0, 0, 0)),
            pl.BlockSpec((None, ch, sd), lambda c: (which(c), 0, 0)),
        ],
        out_specs=(pl.BlockSpec((ch, sd), lambda c: (c, 0)),
                   pl.BlockSpec((ch, sd), lambda c: (jnp.maximum(c - npc, 0), 0))),
        compiler_params=_cparams("arbitrary"),
        name="sgu_spatial",
    )(zs, zs, ln_g.reshape(1, sd), ln_b.reshape(1, sd), wmix, biasfull)


def _sgu_mix_tables(w_s, b_s, dec_seq):
    groups, ch, _ = w_s.shape
    tril = jnp.tril(jnp.ones((ch, ch), bool))
    wp = jnp.where(tril, w_s, 0.0)
    small = jnp.where(jnp.tril(jnp.ones((dec_seq, dec_seq), bool)), w_s[:, :dec_seq, :dec_seq], 0.0)
    reps = ch // dec_seq
    eye = jnp.eye(reps, dtype=w_s.dtype)
    ws = jnp.einsum("ab,gts->gatbs", eye, small).reshape(groups, ch, ch)
    bp = b_s.T
    bs = jnp.tile(b_s[:, :dec_seq].T, (reps, 1))
    return jnp.stack([wp, ws]), jnp.stack([bp, bs])


class _Runs:
    def __init__(self, be, nu, first, run, nxt, last, nruns):
        self.be, self.nu, self.first, self.run, self.nxt, self.last, self.nruns = be, nu, first, run, nxt, last, nruns


N_RUN_SCALARS = 7


def _run_tables(block_e, n_used):
    nb = block_e.shape[0]
    pos = jnp.arange(nb, dtype=jnp.int32)
    live = pos < n_used[0]
    changed = jnp.concatenate([jnp.ones((1,), bool), block_e[1:] != block_e[:-1]]) & live
    run = jnp.cumsum(changed.astype(jnp.int32)) - 1
    nruns = jnp.sum(changed.astype(jnp.int32)).reshape(1)
    start = jnp.where(changed, pos, nb)
    after = jnp.concatenate([start[1:], jnp.full((1,), nb, jnp.int32)])
    nxt_pos = lax.cummin(after[::-1])[::-1]
    last = (nxt_pos >= nb).astype(jnp.int32)
    nxt = block_e[jnp.minimum(nxt_pos, nb - 1)]
    return (block_e, n_used, changed.astype(jnp.int32), run, nxt, last, nruns)


def _stream_expert_weights(runs, copies, on_arrival):
    j, b = pl.program_id(0), pl.program_id(1)
    nj = pl.num_programs(0)

    @pl.when(runs.first[b] == 1)
    def _():
        slot = lax.rem(j * runs.nruns[0] + runs.run[b], 2)

        @pl.when(jnp.logical_and(j == 0, b == 0))
        def _():
            for c in copies(runs.be[b], j, slot):
                c.start()

        for c in copies(runs.be[b], j, slot):
            c.wait()
        on_arrival(slot)

        @pl.when(runs.last[b] == 0)
        def _():
            for c in copies(runs.nxt[b], j, 1 - slot):
                c.start(priority=1)

        @pl.when(jnp.logical_and(runs.last[b] == 1, j + 1 < nj))
        def _():
            for c in copies(runs.be[0], j + 1, 1 - slot):
                c.start(priority=1)


def _start_row_gather(src_hbm, idx_ref, buf, sem):
    for r in range(buf.shape[0]):
        pltpu.make_async_copy(src_hbm.at[pl.ds(idx_ref[0, r], 1), :], buf.at[pl.ds(r, 1), :], sem).start()


def _wait_row_gather(src_hbm, buf, sem):
    for r in range(buf.shape[0]):
        pltpu.make_async_copy(src_hbm.at[pl.ds(0, 1), :], buf.at[pl.ds(r, 1), :], sem).wait()


def _moe_gu_kernel(layer, be, nu, first, run, nxt, last, nruns, cur_idx, nxt_idx, h_hbm, w_hbm, bg_ref, bu_ref,
                   hid_ref, xbuf0, xbuf1, wbuf, wg_bf, wu_bf, xsem, sem):
    runs = _Runs(be, nu, first, run, nxt, last, nruns)
    j, b = pl.program_id(0), pl.program_id(1)
    nj = pl.num_programs(0)
    live = b < nu[0]
    bnh = wg_bf.shape[1]
    de = w_hbm.shape[3] // 2
    xbufs = (xbuf0, xbuf1)

    def copies(e, jj, slot):
        c0 = pl.multiple_of(jj * bnh, LANES)
        return [pltpu.make_async_copy(w_hbm.at[layer, e, :, pl.ds(off + c0, bnh)], wbuf.at[slot, t], sem.at[slot, t])
                for t, off in enumerate((0, de))]

    def on_arrival(slot):
        wg_bf[...] = wbuf[slot, 0].astype(BF16)
        wu_bf[...] = wbuf[slot, 1].astype(BF16)

    @pl.when(jnp.logical_and(j == 0, b == 0))
    def _():
        _start_row_gather(h_hbm, cur_idx, xbuf0, xsem.at[0])

    _stream_expert_weights(runs, copies, on_arrival)

    step = j * nu[0] + b
    is_final = jnp.logical_and(j == nj - 1, b == nu[0] - 1)
    for slot in range(2):
        @pl.when(jnp.logical_and(live, lax.rem(step, 2) == slot))
        def _(slot=slot):
            _wait_row_gather(h_hbm, xbufs[slot], xsem.at[slot])
            _start_row_gather(h_hbm, nxt_idx, xbufs[1 - slot], xsem.at[1 - slot])
            lo, hi = _unpack_bf16_pairs(xbufs[slot][...])
            x = jnp.concatenate([lo, hi], axis=1).astype(BF16)
            gate = jnp.dot(x, wg_bf[...], preferred_element_type=F32) + bg_ref[...]
            up = jnp.dot(x, wu_bf[...], preferred_element_type=F32) + bu_ref[...]
            gate = jnp.minimum(gate, SWIGLU_LIMIT)
            up = jnp.clip(up, -SWIGLU_LIMIT, SWIGLU_LIMIT)
            glu = gate * (1.0 / (1.0 + jnp.exp(-gate * SWIGLU_ALPHA)))
            hid_ref[...] = ((up + 1.0) * glu).astype(hid_ref.dtype)

            @pl.when(is_final)
            def _():
                _wait_row_gather(h_hbm, xbufs[1 - slot], xsem.at[1 - slot])

    @pl.when(jnp.logical_not(live))
    def _():
        hid_ref[...] = jnp.zeros_like(hid_ref)


def _moe_down_kernel(layer, be, nu, first, run, nxt, last, nruns, h_ref, w_hbm, b_ref, y_ref, wbuf, w_bf, sem):
    runs = _Runs(be, nu, first, run, nxt, last, nruns)
    b = pl.program_id(1)
    live = b < nu[0]
    bnd = w_bf.shape[1]

    def copies(e, jj, slot):
        c0 = pl.multiple_of(jj * bnd, LANES)
        return [pltpu.make_async_copy(w_hbm.at[layer, e, :, pl.ds(c0, bnd)], wbuf.at[slot], sem.at[slot])]

    def on_arrival(slot):
        w_bf[...] = wbuf[slot].astype(BF16)

    _stream_expert_weights(runs, copies, on_arrival)

    @pl.when(live)
    def _():
        y_ref[...] = _pack_bf16_pairs(jnp.dot(h_ref[...], w_bf[...], preferred_element_type=F32) + b_ref[...])

    @pl.when(jnp.logical_not(live))
    def _():
        y_ref[...] = jnp.zeros_like(y_ref)


def _moe_experts(h, row_tok, block_e, n_used, layer, w_gu, b_gu, w_down, b_down, bnh):
    n_rows = row_tok.shape[0]
    dh = h.shape[1]
    d = 2 * dh
    depth, n_exp, _, de2 = w_gu.shape
    de = de2 // 2
    bm = MOE_ROWS
    n_blocks = n_rows // bm
    nj = de // bnh
    b_gu4 = b_gu.reshape(depth, n_exp, 1, de2)
    tables = _run_tables(block_e, n_used)
    row_tok3 = row_tok.reshape(n_blocks, 1, bm)
    smem_rows = lambda index_map: pl.BlockSpec((None, 1, bm), index_map, memory_space=pltpu.SMEM)
    hid = pl.pallas_call(
        functools.partial(_moe_gu_kernel, layer),
        out_shape=jax.ShapeDtypeStruct((n_rows, de), BF16),
        grid_spec=pltpu.PrefetchScalarGridSpec(
            num_scalar_prefetch=N_RUN_SCALARS,
            grid=(nj, n_blocks),
            in_specs=[
                smem_rows(lambda j, b, *s: (b, 0, 0)),
                smem_rows(lambda j, b, be, nu, *s: (jnp.where(b + 1 < nu[0], b + 1, 0), 0, 0)),
                pl.BlockSpec(memory_space=pl.ANY),
                pl.BlockSpec(memory_space=pl.ANY),
                pl.BlockSpec((None, None, 1, bnh), lambda j, b, be, *s: (layer, be[b], 0, j)),
                pl.BlockSpec((None, None, 1, bnh), lambda j, b, be, *s: (layer, be[b], 0, nj + j)),
            ],
            out_specs=pl.BlockSpec((bm, bnh), lambda j, b, *s: (b, j)),
            scratch_shapes=[pltpu.VMEM((bm, dh), F32), pltpu.VMEM((bm, dh), F32),
                            pltpu.VMEM((2, 2, d, bnh), F32), pltpu.VMEM((d, bnh), BF16),
                            pltpu.VMEM((d, bnh), BF16), pltpu.SemaphoreType.DMA((2,)),
                            pltpu.SemaphoreType.DMA((2, 2))],
        ),
        compiler_params=_cparams("arbitrary", "arbitrary"),
        name="moe_gate_up",
    )(*tables, row_tok3, row_tok3, h, w_gu, b_gu4, b_gu4)

    return pl.pallas_call(
        functools.partial(_moe_down_kernel, layer),
        out_shape=jax.ShapeDtypeStruct((n_rows, dh), F32),
        grid_spec=pltpu.PrefetchScalarGridSpec(
            num_scalar_prefetch=N_RUN_SCALARS,
            grid=(1, n_blocks),
            in_specs=[
                pl.BlockSpec((bm, de), lambda j, b, *s: (b, 0)),
                pl.BlockSpec(memory_space=pl.ANY),
                pl.BlockSpec((None, None, 1, d), lambda j, b, be, *s: (layer, be[b], 0, 0)),
            ],
            out_specs=pl.BlockSpec((bm, dh), lambda j, b, *s: (b, 0)),
            scratch_shapes=[pltpu.VMEM((2, de, d), F32), pltpu.VMEM((de, d), BF16),
                            pltpu.SemaphoreType.DMA((2,))],
        ),
        compiler_params=_cparams("arbitrary", "arbitrary"),
        name="moe_down",
    )(*tables, hid, w_down, b_down.reshape(depth, n_exp, 1, d))


def _combine_kernel(rows, final, cur_idx, nxt_idx, x_ref, gw_ref, mp_g, ms_g, y_hbm, *refs):
    if final:
        fg_ref, op_ref, os_ref, ybuf0, ybuf1, ysem = refs
    else:
        o_ref, ybuf0, ybuf1, ysem = refs
    ybufs = (ybuf0, ybuf1)
    i = pl.program_id(0)
    nt = pl.num_programs(0)
    rt = x_ref.shape[0]

    @pl.when(i == 0)
    def _():
        _start_row_gather(y_hbm, cur_idx, ybuf0, ysem.at[0])

    def body(slot):
        _wait_row_gather(y_hbm, ybufs[slot], ysem.at[slot])
        _start_row_gather(y_hbm, nxt_idx, ybufs[1 - slot], ysem.at[1 - slot])
        gw = gw_ref[...]
        f_lo = f_hi = None
        for k in range(TOP_K):
            lo, hi = _unpack_bf16_pairs(ybufs[slot][k * rt:(k + 1) * rt, :])
            w = gw[:, k:k + 1]
            f_lo = lo * w if f_lo is None else f_lo + lo * w
            f_hi = hi * w if f_hi is None else f_hi + hi * w
        f = jnp.concatenate([f_lo, f_hi], axis=1)
        x_new = x_ref[...] + _pick_mod(i, rows, mp_g, ms_g) * f
        if not final:
            o_ref[...] = x_new
        else:
            y = _rms(x_new, fg_ref[...])

            @pl.when(i < rows.n_prompt_tiles)
            def _():
                op_ref[...] = y

            @pl.when(i >= rows.n_prompt_tiles)
            def _():
                os_ref[...] = y

        @pl.when(i == nt - 1)
        def _():
            _wait_row_gather(y_hbm, ybufs[1 - slot], ysem.at[1 - slot])

    for slot in range(2):
        pl.when(lax.rem(i, 2) == slot)(functools.partial(body, slot))


def _combine(rows, x, gw, mod_p, mod_s, y, dest, final_g=None):
    m, d = x.shape
    rt = rows.rt
    nt = rows.n_tiles
    final = final_g is not None
    dest_tiles = dest.reshape(nt, rt, TOP_K).transpose(0, 2, 1).reshape(nt, 1, TOP_K * rt)
    smem_rows = lambda index_map: pl.BlockSpec((None, 1, TOP_K * rt), index_map, memory_space=pltpu.SMEM)
    in_specs = [
        smem_rows(lambda i: (i, 0, 0)),
        smem_rows(lambda i: (jnp.where(i + 1 < nt, i + 1, 0), 0, 0)),
        pl.BlockSpec((rt, d), lambda i: (i, 0)),
        pl.BlockSpec((rt, LANES), lambda i: (i, 0)),
        rows.mp_spec(5, d), rows.ms_spec(5, d),
        pl.BlockSpec(memory_space=pl.ANY),
    ]
    args = [dest_tiles, dest_tiles, x, gw, mod_p, mod_s, y]
    scratch = [pltpu.VMEM((TOP_K * rt, d // 2), F32), pltpu.VMEM((TOP_K * rt, d // 2), F32),
               pltpu.SemaphoreType.DMA((2,))]
    out_shape = jax.ShapeDtypeStruct((m, d), F32)
    out_specs = pl.BlockSpec((rt, d), lambda i: (i, 0))
    if final:
        in_specs.append(pl.BlockSpec((1, d), lambda i: (0, 0)))
        args.append(final_g.reshape(1, d))
        npt = rows.n_prompt_tiles
        out_shape = (jax.ShapeDtypeStruct((npt * rt, d), F32), jax.ShapeDtypeStruct((m - npt * rt, d), F32))
        out_specs = (pl.BlockSpec((rt, d), lambda i: (jnp.minimum(i, npt - 1), 0)),
                     pl.BlockSpec((rt, d), lambda i: (jnp.maximum(i - npt, 0), 0)))
    return pl.pallas_call(
        functools.partial(_combine_kernel, rows, final),
        out_shape=out_shape,
        grid=(nt,),
        in_specs=in_specs,
        out_specs=out_specs,
        scratch_shapes=scratch,
        compiler_params=_cparams("arbitrary"),
        name="moe_combine",
    )(*args)


def _moe_layer(rows, x, g2, mod_p, mod_s, layer, router_w, router_b, w_gu, b_gu, w_down, b_down, final_g):
    m, d = x.shape
    h, idx, gw, rank, cnt = _norm_route(rows, x, g2, mod_p, mod_s, router_w[layer], router_b[layer])
    bm = MOE_ROWS
    tk = m * TOP_K
    n_blocks = -(-(tk + N_EXPERTS * (bm - 1)) // bm)
    n_rows = n_blocks * bm
    counts = cnt[0, :N_EXPERTS].astype(jnp.int32)
    padded = (counts + bm - 1) // bm * bm
    pad_end = jnp.cumsum(padded)
    pad_start = pad_end - padded
    dest = pad_start[idx[:, :TOP_K]] + rank[:, :TOP_K]
    tok = jnp.repeat(jnp.arange(m, dtype=jnp.int32), TOP_K)
    row_tok = jnp.zeros((n_rows,), jnp.int32).at[dest.reshape(tk)].set(tok, unique_indices=True)
    block_start = jnp.arange(n_blocks, dtype=jnp.int32) * bm
    block_e = jnp.minimum(jnp.sum(pad_end[None, :] <= block_start[:, None], axis=1), N_EXPERTS - 1).astype(jnp.int32)
    n_used = (pad_end[-1] // bm).astype(jnp.int32).reshape(1)
    de = w_gu.shape[3] // 2
    y = _moe_experts(h, row_tok, block_e, n_used, layer, w_gu, b_gu, w_down, b_down, min(1024, de))
    return _combine(rows.halved(), x, gw, mod_p, mod_s, y, dest, final_g)


def kernel(x_prompt, x_sample, state_gla, c_prompt, c_sample, ada_w, ada_b, norm1_g, norm2_g, final_g,
           gla_w_in, gla_w_gate2, gla_b_gate, gla_onorm_g, gla_w_out,
           sgu_w_in, sgu_b_in, sgu_ln_g, sgu_ln_b, sgu_w_s, sgu_b_s, sgu_w_out, sgu_b_out,
           router_w, router_b, exp_w_gu, exp_b_gu, exp_w_down, exp_b_down):
    batch, seq, d = x_prompt.shape
    dec_batch, dec_seq, _ = x_sample.shape
    depth = ada_w.shape[0]
    m_p, m_s = batch * seq, dec_batch * dec_seq
    m = m_p + m_s
    rt = min(ROW_TILE, m_s)
    rows0 = _Rows(batch, seq, m_s, d, rt)
    kd = d // 2
    n_qkvg = 2 * kd + 2 * d

    n_c = m_s + batch
    n_c_pad = -(-n_c // SUBLANES) * SUBLANES
    c_all = jnp.pad(jnp.concatenate([jnp.repeat(c_sample, dec_seq, axis=0), c_prompt], axis=0),
                    ((0, n_c_pad - n_c), (0, 0)))
    mod_all = _adaln_all(c_all, ada_w, ada_b, min(1024, d))
    mod_s = mod_all

    x = jnp.concatenate([x_prompt.reshape(m_p, d), x_sample.reshape(m_s, d)], axis=0)
    s_out_sample = None
    s_out_prompt, v_out_sample = [], []
    gla_nb = SUBLANES // dec_seq
    gla_ts = min(256, seq)

    for i in range(depth):
        rows = rows0.at_layer(i)
        mod_p = mod_all[i, m_s:n_c].reshape(batch, 6, 1, d)
        h = _norm_mod(rows, x, norm1_g[i], mod_p, mod_s, 0, 1)
        j = i // 2
        if i % 2 == 0:
            z = _matmul(rows, h, gla_w_in, j, n_qkvg, name="gla_in")
            w_a = jnp.pad(gla_w_in[j, :, n_qkvg:], ((0, 0), (0, LANES - GLA_GATE_RANK)))[None]
            a = _matmul(rows, h, w_a, 0, LANES, name="gla_gate_in")
            w2p = jnp.pad(gla_w_gate2[j], ((0, LANES - GLA_GATE_RANK), (0, 0)))
            og, s_p = _gla_prompt(z, a, w2p, gla_b_gate[j], gla_onorm_g[j], batch, seq, m, gla_ts)
            og, s_out_sample = _gla_sample(z, a, w2p, gla_b_gate[j], gla_onorm_g[j], state_gla, j, og,
                                           s_out_sample, m_p, dec_batch, dec_seq, gla_nb)
            s_out_prompt.append(s_p)
            x = _matmul(rows, og, gla_w_out, j, d, epilogue="resid", resid=x, mod_p=mod_p, mod_s=mod_s,
                        which_gate=2, name="gla_out")
        else:
            zs = _matmul(rows, h, sgu_w_in, j, 2 * d, bias=sgu_b_in[j], epilogue="gelu", name="sgu_in")
            wmix, bmix = _sgu_mix_tables(sgu_w_s[j], sgu_b_s[j], dec_seq)
            biasfull = jnp.repeat(bmix, d // SGU_GROUPS, axis=2)
            us, vn_s = _sgu_spatial(zs, sgu_ln_g[j], sgu_ln_b[j], wmix, biasfull, m_p, m_s)
            v_out_sample.append(vn_s.reshape(dec_batch, dec_seq, d))
            x = _matmul(rows, us, sgu_w_out, j, d, bias=sgu_b_out[j], epilogue="resid", resid=x, mod_p=mod_p,
                        mod_s=mod_s, which_gate=2, name="sgu_out")
        x = _moe_layer(rows, x, norm2_g[i], mod_p, mod_s, i, router_w, router_b, exp_w_gu, exp_b_gu,
                       exp_w_down, exp_b_down, final_g if i == depth - 1 else None)

    y_prompt = x[0].reshape(batch, seq, d)
    y_sample = x[1].reshape(dec_batch, dec_seq, d)
    return (y_prompt, y_sample, jnp.stack(s_out_prompt), s_out_sample, jnp.stack(v_out_sample))
```

```python
import functools

import jax
import jax.numpy as jnp
from jax import lax
from jax.experimental import pallas as pl
from jax.experimental.pallas import tpu as pltpu

F32 = jnp.float32
BF16 = jnp.bfloat16
U32 = jnp.uint32

GLA_HEADS = 4
GLA_GATE_RANK = 16
GLA_GATE_NORMALIZER = 16.0
GLA_CHUNK = 32
SGU_GROUPS = 8
SGU_CHUNK = 128
N_EXPERTS = 32
TOP_K = 4
SWIGLU_LIMIT = 7.0
SWIGLU_ALPHA = 1.702
NORM_EPS = 1e-6

LANES = 128
SUBLANES = 8
VMEM_LIMIT = 56 * 1024 * 1024
ROW_TILE = 512
MOE_ROWS = 256
HI = lax.Precision.HIGHEST


def _cparams(*sem):
    return pltpu.CompilerParams(dimension_semantics=sem, vmem_limit_bytes=VMEM_LIMIT)


def _tile(n, target):
    t = min(n, target) // LANES * LANES
    while n % t:
        t -= LANES
    return t


def _bdot(a, b):
    return jnp.dot(a.astype(BF16), b.astype(BF16), preferred_element_type=F32)


def _silu(x):
    return x * (1.0 / (1.0 + jnp.exp(-x)))


def _pack_bf16_pairs(x):
    half = x.shape[1] // 2
    bits = pltpu.bitcast(x.astype(BF16).astype(F32), U32)
    return pltpu.bitcast((bits[:, :half] >> 16) | bits[:, half:], F32)


def _unpack_bf16_pairs(p):
    p = pltpu.bitcast(p, U32)
    return pltpu.bitcast(p << 16, F32), pltpu.bitcast(p & jnp.uint32(0xFFFF0000), F32)


def _adaln_kernel(c_ref, w_ref, b_ref, o_ref):
    o_ref[...] = _bdot(_silu(c_ref[...]), w_ref[...]) + b_ref[...]


def _adaln_all(c_all, ada_w, ada_b, bn):
    depth, d, n6 = ada_w.shape
    rows = c_all.shape[0]
    return pl.pallas_call(
        _adaln_kernel,
        out_shape=jax.ShapeDtypeStruct((depth, rows, n6), F32),
        grid=(depth, n6 // bn),
        in_specs=[
            pl.BlockSpec((rows, d), lambda l, n: (0, 0)),
            pl.BlockSpec((None, d, bn), lambda l, n: (l, 0, n)),
            pl.BlockSpec((None, 1, bn), lambda l, n: (l, 0, n)),
        ],
        out_specs=pl.BlockSpec((None, rows, bn), lambda l, n: (l, 0, n)),
        compiler_params=_cparams("arbitrary", "arbitrary"),
        name="adaln",
    )(c_all, ada_w, ada_b.reshape(depth, 1, n6))


class _Rows:
    def __init__(self, batch, seq, m_sample, d, rt, layer=0):
        self.rt = rt
        self.d = d
        self.layer = layer
        self.tiles_per_batch = seq // rt
        self.n_prompt_tiles = batch * seq // rt
        self.n_tiles = self.n_prompt_tiles + m_sample // rt
        self.batch = batch
        self.halved = lambda: _Rows(batch, seq, m_sample, d, rt // 2, layer)
        self.at_layer = lambda l: _Rows(batch, seq, m_sample, d, rt, l)

    def mp_spec(self, which, width, col=False):
        tpb, nb = self.tiles_per_batch, self.batch
        if not col:
            return pl.BlockSpec((None, None, 1, width), lambda i: (jnp.minimum(i // tpb, nb - 1), which, 0, 0))
        return pl.BlockSpec((None, None, 1, width),
                            lambda n, i: (jnp.minimum(i // tpb, nb - 1), which, 0, n))

    def ms_spec(self, which, width, col=False):
        npt, layer = self.n_prompt_tiles, self.layer
        c0 = which * (self.d // width)
        if not col:
            return pl.BlockSpec((None, self.rt, width), lambda i: (layer, jnp.maximum(i - npt, 0), c0))
        return pl.BlockSpec((None, self.rt, width), lambda n, i: (layer, jnp.maximum(i - npt, 0), c0 + n))


def _pick_mod(i, rows, mp_ref, ms_ref):
    return jnp.where(i >= rows.n_prompt_tiles, ms_ref[...], mp_ref[...])


def _rms(x, g):
    return x * lax.rsqrt(jnp.mean(x * x, axis=-1, keepdims=True) + NORM_EPS) * g


def _norm_mod_kernel(rows, x_ref, g_ref, mp_sh, mp_sc, ms_sh, ms_sc, h_ref):
    i = pl.program_id(0)
    y = _rms(x_ref[...], g_ref[...])
    h = y * (1.0 + _pick_mod(i, rows, mp_sc, ms_sc)) + _pick_mod(i, rows, mp_sh, ms_sh)
    h_ref[...] = h.astype(h_ref.dtype)


def _norm_mod(rows, x, g, mod_p, mod_s, which_shift, which_scale):
    m, d = x.shape
    rt = rows.rt
    return pl.pallas_call(
        functools.partial(_norm_mod_kernel, rows),
        out_shape=jax.ShapeDtypeStruct((m, d), BF16),
        grid=(rows.n_tiles,),
        in_specs=[
            pl.BlockSpec((rt, d), lambda i: (i, 0)),
            pl.BlockSpec((1, d), lambda i: (0, 0)),
            rows.mp_spec(which_shift, d), rows.mp_spec(which_scale, d),
            rows.ms_spec(which_shift, d), rows.ms_spec(which_scale, d),
        ],
        out_specs=pl.BlockSpec((rt, d), lambda i: (i, 0)),
        compiler_params=_cparams("arbitrary"),
        name="norm_mod",
    )(x, g.reshape(1, d), mod_p, mod_p, mod_s, mod_s)


def _lane_pick(lane, k, col, acc):
    return jnp.where(lane == k, col, acc)


def _norm_route_kernel(rows, x_ref, g_ref, mp_sh, mp_sc, ms_sh, ms_sc, rw_ref, rb_ref,
                       h_ref, idx_ref, gw_ref, rank_ref, cnt_ref, carry_ref):
    i = pl.program_id(0)
    rt = rows.rt

    @pl.when(i == 0)
    def _():
        carry_ref[...] = jnp.zeros_like(carry_ref)

    y = _rms(x_ref[...], g_ref[...])
    h = y * (1.0 + _pick_mod(i, rows, mp_sc, ms_sc)) + _pick_mod(i, rows, mp_sh, ms_sh)
    h_ref[...] = _pack_bf16_pairs(h)

    logits = jnp.dot(h, rw_ref[...], preferred_element_type=F32, precision=HI) + rb_ref[...]
    lane = lax.broadcasted_iota(jnp.int32, (rt, LANES), 1)
    neg = jnp.float32(-jnp.inf)
    work = jnp.where(lane < N_EXPERTS, logits, neg)
    member = jnp.zeros((rt, LANES), F32)
    vals, idxs = [], []
    for _ in range(TOP_K):
        mx = jnp.max(work, axis=-1, keepdims=True)
        idx = jnp.min(jnp.where(work == mx, lane, LANES), axis=-1, keepdims=True)
        sel = lane == idx
        member = jnp.where(sel, 1.0, member)
        work = jnp.where(sel, neg, work)
        vals.append(mx)
        idxs.append(idx)
    exps = [jnp.exp(v - vals[0]) for v in vals]
    denom = exps[0]
    for e in exps[1:]:
        denom = denom + e

    r_i = lax.broadcasted_iota(jnp.int32, (rt, rt), 0)
    c_i = lax.broadcasted_iota(jnp.int32, (rt, rt), 1)
    below = jnp.where(r_i > c_i, 1.0, 0.0).astype(BF16)
    before = jnp.dot(below, member.astype(BF16), preferred_element_type=F32) + carry_ref[0:1, :]

    idx_out = jnp.zeros((rt, LANES), jnp.int32)
    gw_out = jnp.zeros((rt, LANES), F32)
    rank_out = jnp.zeros((rt, LANES), jnp.int32)
    for k in range(TOP_K):
        rank_k = jnp.sum(jnp.where(lane == idxs[k], before, 0.0), axis=-1, keepdims=True)
        idx_out = _lane_pick(lane, k, idxs[k], idx_out)
        gw_out = _lane_pick(lane, k, exps[k] / denom, gw_out)
        rank_out = _lane_pick(lane, k, rank_k.astype(jnp.int32), rank_out)
    idx_ref[...] = idx_out
    gw_ref[...] = gw_out
    rank_ref[...] = rank_out

    carry_ref[...] = carry_ref[...] + jnp.sum(member, axis=0, keepdims=True)
    cnt_ref[...] = carry_ref[...]


def _norm_route(rows, x, g, mod_p, mod_s, router_w, router_b):
    m, d = x.shape
    rt = rows.rt
    rw = jnp.pad(router_w, ((0, 0), (0, LANES - N_EXPERTS)))
    rb = jnp.pad(router_b, (0, LANES - N_EXPERTS)).reshape(1, LANES)
    row_spec = pl.BlockSpec((rt, LANES), lambda i: (i, 0))
    return pl.pallas_call(
        functools.partial(_norm_route_kernel, rows),
        out_shape=(
            jax.ShapeDtypeStruct((m, d // 2), F32),
            jax.ShapeDtypeStruct((m, LANES), jnp.int32),
            jax.ShapeDtypeStruct((m, LANES), F32),
            jax.ShapeDtypeStruct((m, LANES), jnp.int32),
            jax.ShapeDtypeStruct((SUBLANES, LANES), F32),
        ),
        grid=(rows.n_tiles,),
        in_specs=[
            pl.BlockSpec((rt, d), lambda i: (i, 0)),
            pl.BlockSpec((1, d), lambda i: (0, 0)),
            rows.mp_spec(3, d), rows.mp_spec(4, d),
            rows.ms_spec(3, d), rows.ms_spec(4, d),
            pl.BlockSpec((d, LANES), lambda i: (0, 0)),
            pl.BlockSpec((1, LANES), lambda i: (0, 0)),
        ],
        out_specs=(
            pl.BlockSpec((rt, d // 2), lambda i: (i, 0)),
            row_spec, row_spec, row_spec,
            pl.BlockSpec((SUBLANES, LANES), lambda i: (0, 0)),
        ),
        scratch_shapes=[pltpu.VMEM((SUBLANES, LANES), F32)],
        compiler_params=_cparams("arbitrary"),
        name="norm_route",
    )(x, g.reshape(1, d), mod_p, mod_p, mod_s, mod_s, rw, rb)


def _gelu(x):
    return 0.5 * x * (1.0 + lax.erf(x * (2.0 ** -0.5)))


def _mm_kernel(rows, epilogue, has_bias, *refs):
    refs = list(refs)
    x_ref, w_ref = refs[0], refs[1]
    pos = 2
    b_ref = None
    if has_bias:
        b_ref = refs[pos]
        pos += 1
    if epilogue == "resid":
        res_ref, mp_g, ms_g = refs[pos:pos + 3]
        pos += 3
    o_ref, wbf_ref = refs[pos], refs[pos + 1]
    i = pl.program_id(1)

    @pl.when(i == 0)
    def _():
        wbf_ref[...] = w_ref[...].astype(BF16)

    acc = jnp.dot(x_ref[...].astype(BF16), wbf_ref[...], preferred_element_type=F32)
    if has_bias:
        acc = acc + b_ref[...]
    if epilogue == "gelu":
        acc = _gelu(acc)
    elif epilogue == "resid":
        acc = res_ref[...] + _pick_mod(i, rows, mp_g, ms_g) * acc
    o_ref[...] = acc.astype(o_ref.dtype)


def _matmul(rows, x, w_all, layer, n, bias=None, epilogue="plain", resid=None, mod_p=None, mod_s=None,
            which_gate=None, bn=1024, out_dtype=F32, name="matmul"):
    m, k = x.shape
    bn = _tile(n, bn)
    rt = rows.rt
    in_specs = [pl.BlockSpec((rt, k), lambda j, i: (i, 0)),
                pl.BlockSpec((None, k, bn), lambda j, i: (layer, 0, j))]
    args = [x, w_all]
    if bias is not None:
        in_specs.append(pl.BlockSpec((1, bn), lambda j, i: (0, j)))
        args.append(bias.reshape(1, n))
    if epilogue == "resid":
        in_specs += [pl.BlockSpec((rt, bn), lambda j, i: (i, j)),
                     rows.mp_spec(which_gate, bn, col=True), rows.ms_spec(which_gate, bn, col=True)]
        args += [resid, mod_p, mod_s]
    return pl.pallas_call(
        functools.partial(_mm_kernel, rows, epilogue, bias is not None),
        out_shape=jax.ShapeDtypeStruct((m, n), out_dtype),
        grid=(n // bn, m // rt),
        in_specs=in_specs,
        out_specs=pl.BlockSpec((rt, bn), lambda j, i: (i, j)),
        scratch_shapes=[pltpu.VMEM((k, bn), BF16)],
        compiler_params=_cparams("arbitrary", "arbitrary"),
        name=name,
    )(*args)


def _log_sigmoid(x):
    return jnp.minimum(x, 0.0) - jnp.log1p(jnp.exp(-jnp.abs(x)))


def _gla_gate_logs(a, w2, b):
    return _log_sigmoid(_bdot(a, w2) + b) * (1.0 / GLA_GATE_NORMALIZER)


def _split3(x):
    hi = x.astype(BF16)
    r1 = x - hi.astype(F32)
    mid = r1.astype(BF16)
    lo = (r1 - mid.astype(F32)).astype(BF16)
    return hi, mid, lo


def _gla_cumsums(lg):
    c = lg.shape[0]
    r_i = lax.broadcasted_iota(jnp.int32, (c, c), 0)
    c_i = lax.broadcasted_iota(jnp.int32, (c, c), 1)
    tri = jnp.where(r_i >= c_i, 1.0, 0.0).astype(BF16)
    ones = jnp.ones((c, LANES), BF16)
    b = None
    tot = None
    for part in _split3(lg):
        pb = jnp.dot(tri, part, preferred_element_type=F32)
        pt = lax.dot_general(part, ones, (((0,), (0,)), ((), ())), preferred_element_type=F32)
        b = pb if b is None else b + pb
        tot = pt if tot is None else tot + pt
    return b, tot


def _gla_chunk(q, k, v, b, b_end_col, s, scale):
    c, dk = q.shape
    dv = v.shape[1]
    r_i = lax.broadcasted_iota(jnp.int32, (c, c), 0)
    c_i = lax.broadcasted_iota(jnp.int32, (c, c), 1)
    causal = r_i >= c_i
    b_end_row = b[c - 1:c, :]
    q_dec = (q * scale) * jnp.exp(b)
    k_inv = k * jnp.exp(-b)
    k_end = k * jnp.exp(b_end_row - b)
    att = lax.dot_general(q_dec.astype(BF16), k_inv.astype(BF16), (((1,), (1,)), ((), ())),
                          preferred_element_type=F32)
    att = jnp.where(causal, att, 0.0)
    o = _bdot(att, v) + _bdot(q_dec, s)
    decay = jnp.exp(b_end_col)
    decay = jnp.concatenate([decay] * (dv // LANES), axis=1)
    kv = lax.dot_general(k_end.astype(BF16), v.astype(BF16), (((0,), (0,)), ((), ())),
                         preferred_element_type=F32)
    return o, s * decay + kv


def _gla_out(o, g, onorm_g):
    return _rms(o, onorm_g) * _silu(g)


def _gla_prompt_kernel(dims, q_ref, k_ref, v_ref, g_ref, a_ref, w2_ref, bg_ref, on_ref,
                       og_ref, sout_ref, s_ref, lg_ref):
    heads, dk, dv, chunk, ts = dims
    step = pl.program_id(1)

    @pl.when(step == 0)
    def _():
        s_ref[...] = jnp.zeros_like(s_ref)

    lg_ref[...] = _gla_gate_logs(a_ref[...], w2_ref[...], bg_ref[...])
    scale = dk ** -0.5

    def body(ci, carry):
        r0 = pl.multiple_of(ci * chunk, chunk)
        rsl = pl.ds(r0, chunk)
        b_all, tot_all = _gla_cumsums(lg_ref[rsl, :])
        for h in range(heads):
            ksl = slice(h * dk, (h + 1) * dk)
            vsl = slice(h * dv, (h + 1) * dv)
            o, s_new = _gla_chunk(q_ref[rsl, ksl], k_ref[rsl, ksl], v_ref[rsl, vsl], b_all[:, ksl],
                                  tot_all[ksl, :], s_ref[h], scale)
            s_ref[h] = s_new
            og_ref[rsl, vsl] = _gla_out(o, g_ref[rsl, vsl], on_ref[...]).astype(og_ref.dtype)
        return carry

    lax.fori_loop(0, ts // chunk, body, 0)

    @pl.when(step == pl.num_programs(1) - 1)
    def _():
        sout_ref[...] = s_ref[...]


def _gla_prompt(z, a, w2p, b_gate, onorm_g, batch, seq, m_total, ts):
    d2 = z.shape[1]
    d = d2 // 3
    kd = d // 2
    heads = GLA_HEADS
    dk, dv = kd // heads, d // heads
    spb = seq // ts
    dims = (heads, dk, dv, GLA_CHUNK, ts)
    row = lambda n, s: n * spb + s
    return pl.pallas_call(
        functools.partial(_gla_prompt_kernel, dims),
        out_shape=(jax.ShapeDtypeStruct((m_total, d), F32),
                   jax.ShapeDtypeStruct((batch, heads, dk, dv), F32)),
        grid=(batch, spb),
        in_specs=[
            pl.BlockSpec((ts, kd), lambda n, s: (row(n, s), 0)),
            pl.BlockSpec((ts, kd), lambda n, s: (row(n, s), 1)),
            pl.BlockSpec((ts, d), lambda n, s: (row(n, s), 1)),
            pl.BlockSpec((ts, d), lambda n, s: (row(n, s), 2)),
            pl.BlockSpec((ts, LANES), lambda n, s: (row(n, s), 0)),
            pl.BlockSpec((LANES, kd), lambda n, s: (0, 0)),
            pl.BlockSpec((1, kd), lambda n, s: (0, 0)),
            pl.BlockSpec((1, dv), lambda n, s: (0, 0)),
        ],
        out_specs=(pl.BlockSpec((ts, d), lambda n, s: (row(n, s), 0)),
                   pl.BlockSpec((None, heads, dk, dv), lambda n, s: (n, 0, 0, 0))),
        scratch_shapes=[pltpu.VMEM((heads, dk, dv), F32), pltpu.VMEM((ts, kd), F32)],
        compiler_params=_cparams("arbitrary", "arbitrary"),
        name="gla_prompt",
    )(z, z, z, z, a, w2p, b_gate.reshape(1, kd), onorm_g.reshape(1, dv))


def _gla_sample_kernel(dims, n_alias, q_ref, k_ref, v_ref, g_ref, a_ref, w2_ref, bg_ref, on_ref, s0_ref, *refs):
    og_ref, sout_ref = refs[n_alias:]
    heads, dk, dv, nb, dec = dims
    c = nb * dec
    scale = dk ** -0.5
    lg_all = _gla_gate_logs(a_ref[...], w2_ref[...], bg_ref[...])
    row = lax.broadcasted_iota(jnp.int32, (c, 1), 0)
    mine = [(row >= b * dec) & (row < (b + 1) * dec) for b in range(nb)]
    sums = [_gla_cumsums(jnp.where(mine[b], lg_all, 0.0)) for b in range(nb)]
    for h in range(heads):
        ksl = slice(h * dk, (h + 1) * dk)
        vsl = slice(h * dv, (h + 1) * dv)
        q, k, v = q_ref[:, ksl], k_ref[:, ksl], v_ref[:, vsl]
        o_all = jnp.zeros((c, dv), F32)
        for b in range(nb):
            z = lambda t: jnp.where(mine[b], t, 0.0)
            o, s_new = _gla_chunk(z(q), z(k), z(v), sums[b][0][:, ksl], sums[b][1][ksl, :], s0_ref[b, h], scale)
            sout_ref[b, h] = s_new
            o_all = jnp.where(mine[b], o, o_all)
        og_ref[:, vsl] = _gla_out(o_all, g_ref[:, vsl], on_ref[...]).astype(og_ref.dtype)


def _gla_sample(z, a, w2p, b_gate, onorm_g, state_all, j, og, s_prev, m_prompt, dec_batch, dec_seq, nb):
    d2 = z.shape[1]
    d = d2 // 3
    kd = d // 2
    heads = GLA_HEADS
    dk, dv = kd // heads, d // heads
    c = nb * dec_seq
    r0 = m_prompt // c
    dims = (heads, dk, dv, nb, dec_seq)
    st_spec = pl.BlockSpec((None, nb, heads, dk, dv), lambda i: (j, i, 0, 0, 0))
    in_specs = [
        pl.BlockSpec((c, kd), lambda i: (r0 + i, 0)),
        pl.BlockSpec((c, kd), lambda i: (r0 + i, 1)),
        pl.BlockSpec((c, d), lambda i: (r0 + i, 1)),
        pl.BlockSpec((c, d), lambda i: (r0 + i, 2)),
        pl.BlockSpec((c, LANES), lambda i: (r0 + i, 0)),
        pl.BlockSpec((LANES, kd), lambda i: (0, 0)),
        pl.BlockSpec((1, kd), lambda i: (0, 0)),
        pl.BlockSpec((1, dv), lambda i: (0, 0)),
        st_spec,
        pl.BlockSpec(memory_space=pl.ANY),
    ]
    args = [z, z, z, z, a, w2p, b_gate.reshape(1, kd), onorm_g.reshape(1, dv), state_all, og]
    aliases = {len(args) - 1: 0}
    if s_prev is not None:
        in_specs.append(pl.BlockSpec(memory_space=pl.ANY))
        args.append(s_prev)
        aliases[len(args) - 1] = 1
    return pl.pallas_call(
        functools.partial(_gla_sample_kernel, dims, len(aliases)),
        out_shape=(jax.ShapeDtypeStruct(og.shape, og.dtype),
                   jax.ShapeDtypeStruct(state_all.shape, state_all.dtype)),
        grid=(dec_batch // nb,),
        in_specs=in_specs,
        out_specs=(pl.BlockSpec((c, d), lambda i: (r0 + i, 0)), st_spec),
        input_output_aliases=aliases,
        compiler_params=_cparams("arbitrary"),
        name="gla_sample",
    )(*args)


def _sgu_kernel(groups, u_ref, v_ref, lg_ref, lb_ref, w_ref, bias_ref, us_ref, vn_ref):
    v = v_ref[...]
    mu = jnp.mean(v, axis=-1, keepdims=True)
    vc = v - mu
    vn = vc * lax.rsqrt(jnp.mean(vc * vc, axis=-1, keepdims=True) + NORM_EPS) * lg_ref[...] + lb_ref[...]
    vn_ref[...] = vn
    gd = v.shape[1] // groups
    for g in range(groups):
        sl = slice(g * gd, (g + 1) * gd)
        s = _bdot(w_ref[g], vn[:, sl]) + bias_ref[:, sl]
        us_ref[:, sl] = (u_ref[:, sl] * s).astype(us_ref.dtype)


def _sgu_spatial(zs, ln_g, ln_b, wmix, biasfull, m_prompt, m_sample):
    m, d2 = zs.shape
    sd = d2 // 2
    ch = SGU_CHUNK
    npc = m_prompt // ch
    which = lambda c: jnp.where(c >= npc, 1, 0)
    return pl.pallas_call(
        functools.partial(_sgu_kernel, SGU_GROUPS),
        out_shape=(jax.ShapeDtypeStruct((m, sd), BF16), jax.ShapeDtypeStruct((m_sample, sd), F32)),
        grid=(m // ch,),
        in_specs=[
            pl.BlockSpec((ch, sd), lambda c: (c, 0)),
            pl.BlockSpec((ch, sd), lambda c: (c, 1)),
            pl.BlockSpec((1, sd), lambda c: (0, 0)),
            pl.BlockSpec((1, sd), lambda c: (0, 0)),
            pl.BlockSpec((None, SGU_GROUPS, ch, ch), lambda c: (which(c), 0, 0, 0)),
            pl.BlockSpec((None, ch, sd), lambda c: (which(c), 0, 0)),
        ],
        out_specs=(pl.BlockSpec((ch, sd), lambda c: (c, 0)),
                   pl.BlockSpec((ch, sd), lambda c: (jnp.maximum(c - npc, 0), 0))),
        compiler_params=_cparams("arbitrary"),
        name="sgu_spatial",
    )(zs, zs, ln_g.reshape(1, sd), ln_b.reshape(1, sd), wmix, biasfull)


def _sgu_mix_tables(w_s, b_s, dec_seq):
    groups, ch, _ = w_s.shape
    tril = jnp.tril(jnp.ones((ch, ch), bool))
    wp = jnp.where(tril, w_s, 0.0)
    small = jnp.where(jnp.tril(jnp.ones((dec_seq, dec_seq), bool)), w_s[:, :dec_seq, :dec_seq], 0.0)
    reps = ch // dec_seq
    eye = jnp.eye(reps, dtype=w_s.dtype)
    ws = jnp.einsum("ab,gts->gatbs", eye, small).reshape(groups, ch, ch)
    bp = b_s.T
    bs = jnp.tile(b_s[:, :dec_seq].T, (reps, 1))
    return jnp.stack([wp, ws]), jnp.stack([bp, bs])


class _Runs:
    def __init__(self, be, nu, first, run, nxt, last, nruns):
        self.be, self.nu, self.first, self.run, self.nxt, self.last, self.nruns = be, nu, first, run, nxt, last, nruns


N_RUN_SCALARS = 7


def _run_tables(block_e, n_used):
    nb = block_e.shape[0]
    pos = jnp.arange(nb, dtype=jnp.int32)
    live = pos < n_used[0]
    changed = jnp.concatenate([jnp.ones((1,), bool), block_e[1:] != block_e[:-1]]) & live
    run = jnp.cumsum(changed.astype(jnp.int32)) - 1
    nruns = jnp.sum(changed.astype(jnp.int32)).reshape(1)
    start = jnp.where(changed, pos, nb)
    after = jnp.concatenate([start[1:], jnp.full((1,), nb, jnp.int32)])
    nxt_pos = lax.cummin(after[::-1])[::-1]
    last = (nxt_pos >= nb).astype(jnp.int32)
    nxt = block_e[jnp.minimum(nxt_pos, nb - 1)]
    return (block_e, n_used, changed.astype(jnp.int32), run, nxt, last, nruns)


def _stream_expert_weights(runs, copies, on_arrival):
    j, b = pl.program_id(0), pl.program_id(1)
    nj = pl.num_programs(0)

    @pl.when(runs.first[b] == 1)
    def _():
        slot = lax.rem(j * runs.nruns[0] + runs.run[b], 2)

        @pl.when(jnp.logical_and(j == 0, b == 0))
        def _():
            for c in copies(runs.be[b], j, slot):
                c.start()

        for c in copies(runs.be[b], j, slot):
            c.wait()
        on_arrival(slot)

        @pl.when(runs.last[b] == 0)
        def _():
            for c in copies(runs.nxt[b], j, 1 - slot):
                c.start(priority=1)

        @pl.when(jnp.logical_and(runs.last[b] == 1, j + 1 < nj))
        def _():
            for c in copies(runs.be[0], j + 1, 1 - slot):
                c.start(priority=1)


def _start_row_gather(src_hbm, idx_ref, buf, sem):
    for r in range(buf.shape[0]):
        pltpu.make_async_copy(src_hbm.at[pl.ds(idx_ref[0, r], 1), :], buf.at[pl.ds(r, 1), :], sem).start()


def _wait_row_gather(src_hbm, buf, sem):
    for r in range(buf.shape[0]):
        pltpu.make_async_copy(src_hbm.at[pl.ds(0, 1), :], buf.at[pl.ds(r, 1), :], sem).wait()


def _moe_gu_kernel(layer, be, nu, first, run, nxt, last, nruns, x_ref, w_hbm, bg_ref, bu_ref,
                   hid_ref, wbuf, wg_bf, wu_bf, sem):
    runs = _Runs(be, nu, first, run, nxt, last, nruns)
    b = pl.program_id(1)
    live = b < nu[0]
    bnh = wg_bf.shape[1]
    de = w_hbm.shape[3] // 2

    def copies(e, jj, slot):
        c0 = pl.multiple_of(jj * bnh, LANES)
        return [pltpu.make_async_copy(w_hbm.at[layer, e, :, pl.ds(off + c0, bnh)], wbuf.at[slot, t], sem.at[slot, t])
                for t, off in enumerate((0, de))]

    def on_arrival(slot):
        wg_bf[...] = wbuf[slot, 0].astype(BF16)
        wu_bf[...] = wbuf[slot, 1].astype(BF16)

    _stream_expert_weights(runs, copies, on_arrival)

    @pl.when(live)
    def _():
        lo, hi = _unpack_bf16_pairs(x_ref[...])
        x = jnp.concatenate([lo, hi], axis=1).astype(BF16)
        gate = jnp.dot(x, wg_bf[...], preferred_element_type=F32) + bg_ref[...]
        up = jnp.dot(x, wu_bf[...], preferred_element_type=F32) + bu_ref[...]
        gate = jnp.minimum(gate, SWIGLU_LIMIT)
        up = jnp.clip(up, -SWIGLU_LIMIT, SWIGLU_LIMIT)
        glu = gate * (1.0 / (1.0 + jnp.exp(-gate * SWIGLU_ALPHA)))
        hid_ref[...] = ((up + 1.0) * glu).astype(hid_ref.dtype)

    @pl.when(jnp.logical_not(live))
    def _():
        hid_ref[...] = jnp.zeros_like(hid_ref)


def _moe_down_kernel(layer, be, nu, first, run, nxt, last, nruns, h_ref, w_hbm, b_ref, y_ref, wbuf, w_bf, sem):
    runs = _Runs(be, nu, first, run, nxt, last, nruns)
    b = pl.program_id(1)
    live = b < nu[0]
    bnd = w_bf.shape[1]

    def copies(e, jj, slot):
        c0 = pl.multiple_of(jj * bnd, LANES)
        return [pltpu.make_async_copy(w_hbm.at[layer, e, :, pl.ds(c0, bnd)], wbuf.at[slot], sem.at[slot])]

    def on_arrival(slot):
        w_bf[...] = wbuf[slot].astype(BF16)

    _stream_expert_weights(runs, copies, on_arrival)

    @pl.when(live)
    def _():
        y_ref[...] = _pack_bf16_pairs(jnp.dot(h_ref[...], w_bf[...], preferred_element_type=F32) + b_ref[...])

    @pl.when(jnp.logical_not(live))
    def _():
        y_ref[...] = jnp.zeros_like(y_ref)


def _moe_experts(x_sorted, block_e, n_used, layer, w_gu, b_gu, w_down, b_down, bnh):
    n_rows, dh = x_sorted.shape
    d = 2 * dh
    depth, n_exp, _, de2 = w_gu.shape
    de = de2 // 2
    bm = MOE_ROWS
    n_blocks = n_rows // bm
    nj = de // bnh
    b_gu4 = b_gu.reshape(depth, n_exp, 1, de2)
    tables = _run_tables(block_e, n_used)
    hid = pl.pallas_call(
        functools.partial(_moe_gu_kernel, layer),
        out_shape=jax.ShapeDtypeStruct((n_rows, de), BF16),
        grid_spec=pltpu.PrefetchScalarGridSpec(
            num_scalar_prefetch=N_RUN_SCALARS,
            grid=(nj, n_blocks),
            in_specs=[
                pl.BlockSpec((bm, dh), lambda j, b, *s: (b, 0)),
                pl.BlockSpec(memory_space=pl.ANY),
                pl.BlockSpec((None, None, 1, bnh), lambda j, b, be, *s: (layer, be[b], 0, j)),
                pl.BlockSpec((None, None, 1, bnh), lambda j, b, be, *s: (layer, be[b], 0, nj + j)),
            ],
            out_specs=pl.BlockSpec((bm, bnh), lambda j, b, *s: (b, j)),
            scratch_shapes=[pltpu.VMEM((2, 2, d, bnh), F32), pltpu.VMEM((d, bnh), BF16),
                            pltpu.VMEM((d, bnh), BF16), pltpu.SemaphoreType.DMA((2, 2))],
        ),
        compiler_params=_cparams("arbitrary", "arbitrary"),
        name="moe_gate_up",
    )(*tables, x_sorted, w_gu, b_gu4, b_gu4)

    return pl.pallas_call(
        functools.partial(_moe_down_kernel, layer),
        out_shape=jax.ShapeDtypeStruct((n_rows, dh), F32),
        grid_spec=pltpu.PrefetchScalarGridSpec(
            num_scalar_prefetch=N_RUN_SCALARS,
            grid=(1, n_blocks),
            in_specs=[
                pl.BlockSpec((bm, de), lambda j, b, *s: (b, 0)),
                pl.BlockSpec(memory_space=pl.ANY),
                pl.BlockSpec((None, None, 1, d), lambda j, b, be, *s: (layer, be[b], 0, 0)),
            ],
            out_specs=pl.BlockSpec((bm, dh), lambda j, b, *s: (b, 0)),
            scratch_shapes=[pltpu.VMEM((2, de, d), F32), pltpu.VMEM((de, d), BF16),
                            pltpu.SemaphoreType.DMA((2,))],
        ),
        compiler_params=_cparams("arbitrary", "arbitrary"),
        name="moe_down",
    )(*tables, hid, w_down, b_down.reshape(depth, n_exp, 1, d))


def _combine_kernel(rows, final, cur_idx, nxt_idx, x_ref, gw_ref, mp_g, ms_g, y_hbm, *refs):
    if final:
        fg_ref, op_ref, os_ref, ybuf0, ybuf1, ysem = refs
    else:
        o_ref, ybuf0, ybuf1, ysem = refs
    ybufs = (ybuf0, ybuf1)
    i = pl.program_id(0)
    nt = pl.num_programs(0)
    rt = x_ref.shape[0]

    @pl.when(i == 0)
    def _():
        _start_row_gather(y_hbm, cur_idx, ybuf0, ysem.at[0])

    def body(slot):
        _wait_row_gather(y_hbm, ybufs[slot], ysem.at[slot])
        _start_row_gather(y_hbm, nxt_idx, ybufs[1 - slot], ysem.at[1 - slot])
        gw = gw_ref[...]
        f_lo = f_hi = None
        for k in range(TOP_K):
            lo, hi = _unpack_bf16_pairs(ybufs[slot][k * rt:(k + 1) * rt, :])
            w = gw[:, k:k + 1]
            f_lo = lo * w if f_lo is None else f_lo + lo * w
            f_hi = hi * w if f_hi is None else f_hi + hi * w
        f = jnp.concatenate([f_lo, f_hi], axis=1)
        x_new = x_ref[...] + _pick_mod(i, rows, mp_g, ms_g) * f
        if not final:
            o_ref[...] = x_new
        else:
            y = _rms(x_new, fg_ref[...])

            @pl.when(i < rows.n_prompt_tiles)
            def _():
                op_ref[...] = y

            @pl.when(i >= rows.n_prompt_tiles)
            def _():
                os_ref[...] = y

        @pl.when(i == nt - 1)
        def _():
            _wait_row_gather(y_hbm, ybufs[1 - slot], ysem.at[1 - slot])

    for slot in range(2):
        pl.when(lax.rem(i, 2) == slot)(functools.partial(body, slot))


def _combine(rows, x, gw, mod_p, mod_s, y, dest, final_g=None):
    m, d = x.shape
    rt = rows.rt
    nt = rows.n_tiles
    final = final_g is not None
    dest_tiles = dest.reshape(nt, rt, TOP_K).transpose(0, 2, 1).reshape(nt, 1, TOP_K * rt)
    smem_rows = lambda index_map: pl.BlockSpec((None, 1, TOP_K * rt), index_map, memory_space=pltpu.SMEM)
    in_specs = [
        smem_rows(lambda i: (i, 0, 0)),
        smem_rows(lambda i: (jnp.where(i + 1 < nt, i + 1, 0), 0, 0)),
        pl.BlockSpec((rt, d), lambda i: (i, 0)),
        pl.BlockSpec((rt, LANES), lambda i: (i, 0)),
        rows.mp_spec(5, d), rows.ms_spec(5, d),
        pl.BlockSpec(memory_space=pl.ANY),
    ]
    args = [dest_tiles, dest_tiles, x, gw, mod_p, mod_s, y]
    scratch = [pltpu.VMEM((TOP_K * rt, d // 2), F32), pltpu.VMEM((TOP_K * rt, d // 2), F32),
               pltpu.SemaphoreType.DMA((2,))]
    out_shape = jax.ShapeDtypeStruct((m, d), F32)
    out_specs = pl.BlockSpec((rt, d), lambda i: (i, 0))
    if final:
        in_specs.append(pl.BlockSpec((1, d), lambda i: (0, 0)))
        args.append(final_g.reshape(1, d))
        npt = rows.n_prompt_tiles
        out_shape = (jax.ShapeDtypeStruct((npt * rt, d), F32), jax.ShapeDtypeStruct((m - npt * rt, d), F32))
        out_specs = (pl.BlockSpec((rt, d), lambda i: (jnp.minimum(i, npt - 1), 0)),
                     pl.BlockSpec((rt, d), lambda i: (jnp.maximum(i - npt, 0), 0)))
    return pl.pallas_call(
        functools.partial(_combine_kernel, rows, final),
        out_shape=out_shape,
        grid=(nt,),
        in_specs=in_specs,
        out_specs=out_specs,
        scratch_shapes=scratch,
        compiler_params=_cparams("arbitrary"),
        name="moe_combine",
    )(*args)


def _moe_layer(rows, x, g2, mod_p, mod_s, layer, router_w, router_b, w_gu, b_gu, w_down, b_down, final_g):
    m, d = x.shape
    h, idx, gw, rank, cnt = _norm_route(rows, x, g2, mod_p, mod_s, router_w[layer], router_b[layer])
    bm = MOE_ROWS
    tk = m * TOP_K
    n_blocks = -(-(tk + N_EXPERTS * (bm - 1)) // bm)
    n_rows = n_blocks * bm
    counts = cnt[0, :N_EXPERTS].astype(jnp.int32)
    padded = (counts + bm - 1) // bm * bm
    pad_end = jnp.cumsum(padded)
    pad_start = pad_end - padded
    dest = pad_start[idx[:, :TOP_K]] + rank[:, :TOP_K]
    tok = jnp.repeat(jnp.arange(m, dtype=jnp.int32), TOP_K)
    row_tok = jnp.zeros((n_rows,), jnp.int32).at[dest.reshape(tk)].set(tok, unique_indices=True)
    block_start = jnp.arange(n_blocks, dtype=jnp.int32) * bm
    block_e = jnp.minimum(jnp.sum(pad_end[None, :] <= block_start[:, None], axis=1), N_EXPERTS - 1).astype(jnp.int32)
    n_used = (pad_end[-1] // bm).astype(jnp.int32).reshape(1)
    de = w_gu.shape[3] // 2
    x_sorted = jnp.take(h, row_tok, axis=0, mode="clip")
    y = _moe_experts(x_sorted, block_e, n_used, layer, w_gu, b_gu, w_down, b_down, min(1024, de))
    return _combine(rows.halved(), x, gw, mod_p, mod_s, y, dest, final_g)


def kernel(x_prompt, x_sample, state_gla, c_prompt, c_sample, ada_w, ada_b, norm1_g, norm2_g, final_g,
           gla_w_in, gla_w_gate2, gla_b_gate, gla_onorm_g, gla_w_out,
           sgu_w_in, sgu_b_in, sgu_ln_g, sgu_ln_b, sgu_w_s, sgu_b_s, sgu_w_out, sgu_b_out,
           router_w, router_b, exp_w_gu, exp_b_gu, exp_w_down, exp_b_down):
    batch, seq, d = x_prompt.shape
    dec_batch, dec_seq, _ = x_sample.shape
    depth = ada_w.shape[0]
    m_p, m_s = batch * seq, dec_batch * dec_seq
    m = m_p + m_s
    rt = min(ROW_TILE, m_s)
    rows0 = _Rows(batch, seq, m_s, d, rt)
    kd = d // 2
    n_qkvg = 2 * kd + 2 * d

    n_c = m_s + batch
    n_c_pad = -(-n_c // SUBLANES) * SUBLANES
    c_all = jnp.pad(jnp.concatenate([jnp.repeat(c_sample, dec_seq, axis=0), c_prompt], axis=0),
                    ((0, n_c_pad - n_c), (0, 0)))
    mod_all = _adaln_all(c_all, ada_w, ada_b, min(1024, d))
    mod_s = mod_all

    x = jnp.concatenate([x_prompt.reshape(m_p, d), x_sample.reshape(m_s, d)], axis=0)
    s_out_sample = None
    s_out_prompt, v_out_sample = [], []
    gla_nb = SUBLANES // dec_seq
    gla_ts = min(256, seq)

    for i in range(depth):
        rows = rows0.at_layer(i)
        mod_p = mod_all[i, m_s:n_c].reshape(batch, 6, 1, d)
        h = _norm_mod(rows, x, norm1_g[i], mod_p, mod_s, 0, 1)
        j = i // 2
        if i % 2 == 0:
            z = _matmul(rows, h, gla_w_in, j, n_qkvg, name="gla_in")
            w_a = jnp.pad(gla_w_in[j, :, n_qkvg:], ((0, 0), (0, LANES - GLA_GATE_RANK)))[None]
            a = _matmul(rows, h, w_a, 0, LANES, name="gla_gate_in")
            w2p = jnp.pad(gla_w_gate2[j], ((0, LANES - GLA_GATE_RANK), (0, 0)))
            og, s_p = _gla_prompt(z, a, w2p, gla_b_gate[j], gla_onorm_g[j], batch, seq, m, gla_ts)
            og, s_out_sample = _gla_sample(z, a, w2p, gla_b_gate[j], gla_onorm_g[j], state_gla, j, og,
                                           s_out_sample, m_p, dec_batch, dec_seq, gla_nb)
            s_out_prompt.append(s_p)
            x = _matmul(rows, og, gla_w_out, j, d, epilogue="resid", resid=x, mod_p=mod_p, mod_s=mod_s,
                        which_gate=2, name="gla_out")
        else:
            zs = _matmul(rows, h, sgu_w_in, j, 2 * d, bias=sgu_b_in[j], epilogue="gelu", name="sgu_in")
            wmix, bmix = _sgu_mix_tables(sgu_w_s[j], sgu_b_s[j], dec_seq)
            biasfull = jnp.repeat(bmix, d // SGU_GROUPS, axis=2)
            us, vn_s = _sgu_spatial(zs, sgu_ln_g[j], sgu_ln_b[j], wmix, biasfull, m_p, m_s)
            v_out_sample.append(vn_s.reshape(dec_batch, dec_seq, d))
            x = _matmul(rows, us, sgu_w_out, j, d, bias=sgu_b_out[j], epilogue="resid", resid=x, mod_p=mod_p,
                        mod_s=mod_s, which_gate=2, name="sgu_out")
        x = _moe_layer(rows, x, norm2_g[i], mod_p, mod_s, i, router_w, router_b, exp_w_gu, exp_b_gu,
                       exp_w_down, exp_b_down, final_g if i == depth - 1 else None)

    y_prompt = x[0].reshape(batch, seq, d)
    y_sample = x[1].reshape(dec_batch, dec_seq, d)
    return (y_prompt, y_sample, jnp.stack(s_out_prompt), s_out_sample, jnp.stack(v_out_sample))
```
